```python
import math
import jax
import jax.numpy as jnp
from jax import lax
import numpy as np

D_MODEL = 1024
BATCH = 4
SEQ = 8192
DEPTH = 1
DEC_BATCH = 32
DEC_SEQ = 64
PAST_LEN = 1024

CHUNK = 64
SSM_WIDTH = 1024
SSM_GROUP = 16
SSM_GROUPS = SSM_WIDTH // SSM_GROUP
SSM_STATE = 64
SSM_MAX_RE = -1e-4
HEAD_DIM = 64
N_HEADS = D_MODEL // (2 * HEAD_DIM)
ATTN_WIDTH = N_HEADS * 2 * HEAD_DIM
ROPE_THETA = 10000.0
Q_BLOCK = 128
MASK_VALUE = -1e30
N_GROUPS = 4
EXPERTS_PER_GROUP = 8
N_EXPERTS = N_GROUPS * EXPERTS_PER_GROUP
D_EXPERT = D_MODEL // 4
TOP_K = 2
IN_WIDTH = SSM_WIDTH + 3 * ATTN_WIDTH + 2 * D_MODEL
SPLITS = (SSM_WIDTH, SSM_WIDTH + ATTN_WIDTH, SSM_WIDTH + 2 * ATTN_WIDTH, SSM_WIDTH + 3 * ATTN_WIDTH)
DEEPNORM_ALPHA = (2.0 * DEPTH) ** 0.25
DEEPNORM_BETA = (8.0 * DEPTH) ** -0.25
LN_EPS = 1e-5

kernel_name = "hybrid_s5_diffattn_hmoe_stream_step"


def lambda_init_for(layer_idx):
    return 0.8 - 0.6 * math.exp(-0.3 * layer_idx)


def layer_norm(x, g, b):
    xf = x.astype(jnp.float32)
    mu = jnp.mean(xf, axis=-1, keepdims=True)
    var = jnp.mean(jnp.square(xf - mu), axis=-1, keepdims=True)
    y = (xf - mu) * lax.rsqrt(var + LN_EPS) * g.astype(jnp.float32) + b.astype(jnp.float32)
    return y.astype(x.dtype)


def rms_norm(x, g):
    xf = x.astype(jnp.float32)
    y = xf * lax.rsqrt(jnp.mean(jnp.square(xf), axis=-1, keepdims=True) + LN_EPS) * g.astype(jnp.float32)
    return y.astype(x.dtype)


def rope(x, pos):
    half = HEAD_DIM // 2
    inv_freq = ROPE_THETA ** (-jnp.arange(half, dtype=jnp.float32) / half)
    ang = pos.astype(jnp.float32)[:, None] * inv_freq[None, :]
    cos = jnp.cos(ang)[:, None, None, :]
    sin = jnp.sin(ang)[:, None, None, :]
    xf = x.astype(jnp.float32)
    x1, x2 = xf[..., :half], xf[..., half:]
    return jnp.concatenate([x1 * cos - x2 * sin, x1 * sin + x2 * cos], axis=-1).astype(x.dtype)


def diff_attend(q, k, v, lam, mask):
    s = jnp.einsum('bqhmd,bkhmd->bhmqk', q, k).astype(jnp.float32) * (HEAD_DIM ** -0.5)
    if mask is not None:
        s = jnp.where(mask, s, MASK_VALUE)
    p = jax.nn.softmax(s, axis=-1)
    a = p[:, :, 0] - lam * p[:, :, 1]
    return jnp.einsum('bhqk,bkhe->bqhe', a.astype(v.dtype), v)


def diff_attn_prompt(q, k, v, lam):
    bsz, s_len = q.shape[0], q.shape[1]
    nb = s_len // Q_BLOCK
    q_blocks = jnp.moveaxis(q.reshape(bsz, nb, Q_BLOCK, N_HEADS, 2, HEAD_DIM), 1, 0)
    key_chunk = jnp.arange(s_len, dtype=jnp.int32) // CHUNK

    def one_block(args):
        q_blk, blk = args
        query_chunk = (blk * Q_BLOCK + jnp.arange(Q_BLOCK, dtype=jnp.int32)) // CHUNK
        mask = key_chunk[None, :] <= query_chunk[:, None]
        return diff_attend(q_blk, k, v, lam, mask)

    out = lax.map(one_block, (q_blocks, jnp.arange(nb, dtype=jnp.int32)))
    return jnp.moveaxis(out, 0, 1).reshape(bsz, s_len, N_HEADS, 2 * HEAD_DIM)


def s5_branch(u, s_re, s_im, p):
    f32 = jnp.float32
    bsz, t = u.shape[0], u.shape[1]
    ug = u.astype(f32).reshape(bsz, t, SSM_GROUPS, SSM_GROUP)
    lam = lax.complex(jnp.minimum(p['lambda_re'].astype(f32), SSM_MAX_RE), p['lambda_im'].astype(f32))
    step = jnp.exp(p['log_step'].astype(f32))[:, None]
    a_bar = jnp.exp(lam * step)
    b_mat = lax.complex(p['b_re'].astype(f32), p['b_im'].astype(f32))
    b_bar = ((a_bar - 1.0) / lam)[..., None] * b_mat
    bu = jnp.einsum('gpc,btgc->btgp', b_bar, ug.astype(jnp.complex64))
    s0 = lax.complex(s_re.astype(f32), s_im.astype(f32))
    bu = bu.at[:, 0].add(a_bar[None] * s0)
    a_seq = jnp.broadcast_to(a_bar[None, None], (1, t, SSM_GROUPS, SSM_STATE))

    def combine(e1, e2):
        a1, b1 = e1
        a2, b2 = e2
        return a1 * a2, a2 * b1 + b2

    _, states = lax.associative_scan(combine, (a_seq, bu), axis=1)
    y = (jnp.einsum('gcp,btgp->btgc', p['c_re'].astype(f32), states.real)
         - jnp.einsum('gcp,btgp->btgc', p['c_im'].astype(f32), states.imag)
         + p['d_skip'].astype(f32) * ug)
    y = jax.nn.gelu(y.reshape(bsz, t, SSM_WIDTH)).astype(u.dtype)
    a, g = jnp.split(y @ p['w_glu'], 2, axis=-1)
    last = states[:, -1]
    return a * jax.nn.sigmoid(g), last.real, last.imag


def token_mixer(x, pos, k_past, v_past, s_re, s_im, p, lam_init):
    f32 = jnp.float32
    bsz, t = x.shape[0], x.shape[1]
    proj = x @ p['w_in']
    u, q, k, v, gates = jnp.split(proj, SPLITS, axis=-1)
    q = rope(q.reshape(bsz, t, N_HEADS, 2, HEAD_DIM), pos)
    k = rope(k.reshape(bsz, t, N_HEADS, 2, HEAD_DIM), pos)
    v = v.reshape(bsz, t, N_HEADS, 2 * HEAD_DIM)
    lam = (jnp.exp(jnp.sum(p['lambda_q1'].astype(f32) * p['lambda_k1'].astype(f32)))
           - jnp.exp(jnp.sum(p['lambda_q2'].astype(f32) * p['lambda_k2'].astype(f32)))
           + lam_init)
    if k_past is None:
        attn = diff_attn_prompt(q, k, v, lam)
    else:
        k_all = jnp.concatenate([k_past.astype(k.dtype), k], axis=1)
        v_all = jnp.concatenate([v_past.astype(v.dtype), v], axis=1)
        attn = diff_attend(q, k_all, v_all, lam, None)
    attn = rms_norm(attn, p['subln_g']) * (1.0 - lam_init)
    b_attn = attn.reshape(bsz, t, ATTN_WIDTH) @ p['w_attn_branch']
    b_ssm, s_re_new, s_im_new = s5_branch(u, s_re, s_im, p)
    g_ssm, g_attn = jnp.split(gates, 2, axis=-1)
    merged = jax.nn.sigmoid(g_ssm) * b_ssm + jax.nn.sigmoid(g_attn) * b_attn
    return merged @ p['w_out'], k, v, s_re_new, s_im_new


def hier_moe(x, p):
    f32 = jnp.float32
    n = x.shape[0]
    xf = x.astype(f32)
    group_logits = xf @ p['w_router_group'].astype(f32)
    group_prob = jax.nn.softmax(group_logits, axis=-1)
    g_idx = jnp.argmax(group_logits, axis=-1)
    rows = jnp.arange(n)
    g_weight = group_prob[rows, g_idx][:, None]
    expert_logits = jnp.einsum('nd,dge->nge', xf, p['w_router_expert'].astype(f32))
    within = expert_logits[rows, g_idx]
    top_val, top_idx = lax.top_k(within, TOP_K)
    top_w = jax.nn.softmax(top_val, axis=-1) * g_weight
    expert_id = g_idx[:, None] * EXPERTS_PER_GROUP + top_idx
    gate = jnp.sum(jax.nn.one_hot(expert_id, N_EXPERTS, dtype=f32) * top_w[..., None], axis=1)
    y = jnp.zeros_like(x)
    for e in range(N_EXPERTS):
        h = jax.nn.silu(x @ p['w1'][e]) * (x @ p['w3'][e])
        y = y + gate[:, e:e + 1].astype(x.dtype) * (h @ p['w2'][e])
    return y


def encoder_layer(x, pos, k_past, v_past, s_re, s_im, p, lam_init):
    mix, k_new, v_new, s_re_new, s_im_new = token_mixer(x, pos, k_past, v_past, s_re, s_im, p, lam_init)
    h = layer_norm(DEEPNORM_ALPHA * x + mix, p['ln1_g'], p['ln1_b'])
    bsz, t = h.shape[0], h.shape[1]
    ffn = hier_moe(h.reshape(bsz * t, D_MODEL), p).reshape(bsz, t, D_MODEL)
    h = layer_norm(DEEPNORM_ALPHA * h + ffn, p['ln2_g'], p['ln2_b'])
    return h, k_new, v_new, s_re_new, s_im_new


def setup_inputs(seed: int = 0) -> dict:
    key = jax.random.key(seed)
    ks = jax.random.split(key, 40)
    f32 = jnp.float32

    def nrm(k, shape, scale):
        return scale * jax.random.normal(k, shape, f32)

    L = DEPTH
    x_prompt = nrm(ks[0], (BATCH, SEQ, D_MODEL), 1.0)
    x_sample = nrm(ks[1], (DEC_BATCH, DEC_SEQ, D_MODEL), 1.0)
    cache_k = nrm(ks[2], (L, DEC_BATCH, PAST_LEN, N_HEADS, 2, HEAD_DIM), 1.0)
    cache_v = nrm(ks[3], (L, DEC_BATCH, PAST_LEN, N_HEADS, 2 * HEAD_DIM), 1.0)
    state_ssm_re = nrm(ks[4], (L, DEC_BATCH, SSM_GROUPS, SSM_STATE), 0.5)
    state_ssm_im = nrm(ks[5], (L, DEC_BATCH, SSM_GROUPS, SSM_STATE), 0.5)
    w_in = jnp.concatenate([
        nrm(ks[6], (L, D_MODEL, SSM_WIDTH + 2 * ATTN_WIDTH), D_MODEL ** -0.5),
        nrm(ks[7], (L, D_MODEL, ATTN_WIDTH), DEEPNORM_BETA * D_MODEL ** -0.5),
        nrm(ks[8], (L, D_MODEL, 2 * D_MODEL), D_MODEL ** -0.5)], axis=-1)
    lambda_re = -0.5 + nrm(ks[9], (L, SSM_GROUPS, SSM_STATE), 0.01)
    lambda_im = math.pi * jnp.arange(SSM_STATE, dtype=f32) + nrm(ks[10], (L, SSM_GROUPS, SSM_STATE), 0.01)
    log_step = jax.random.uniform(ks[11], (L, SSM_GROUPS), f32, math.log(1e-3), math.log(1e-1))
    b_re = nrm(ks[12], (L, SSM_GROUPS, SSM_STATE, SSM_GROUP), (2 * SSM_GROUP) ** -0.5)
    b_im = nrm(ks[13], (L, SSM_GROUPS, SSM_STATE, SSM_GROUP), (2 * SSM_GROUP) ** -0.5)
    c_re = nrm(ks[14], (L, SSM_GROUPS, SSM_GROUP, SSM_STATE), (2 * SSM_STATE) ** -0.5)
    c_im = nrm(ks[15], (L, SSM_GROUPS, SSM_GROUP, SSM_STATE), (2 * SSM_STATE) ** -0.5)
    d_skip = nrm(ks[16], (L, SSM_GROUPS, SSM_GROUP), 1.0)
    w_glu = nrm(ks[17], (L, SSM_WIDTH, 2 * D_MODEL), SSM_WIDTH ** -0.5)
    lambda_q1 = nrm(ks[18], (L, HEAD_DIM), 0.1)
    lambda_k1 = nrm(ks[19], (L, HEAD_DIM), 0.1)
    lambda_q2 = nrm(ks[20], (L, HEAD_DIM), 0.1)
    lambda_k2 = nrm(ks[21], (L, HEAD_DIM), 0.1)
    subln_g = 1.0 + nrm(ks[22], (L, 2 * HEAD_DIM), 0.02)
    w_attn_branch = nrm(ks[23], (L, ATTN_WIDTH, D_MODEL), ATTN_WIDTH ** -0.5)
    w_out = nrm(ks[24], (L, D_MODEL, D_MODEL), DEEPNORM_BETA * D_MODEL ** -0.5)
    ln1_g = 1.0 + nrm(ks[25], (L, D_MODEL), 0.02)
    ln1_b = nrm(ks[26], (L, D_MODEL), 0.02)
    w_router_group = nrm(ks[27], (L, D_MODEL, N_GROUPS), D_MODEL ** -0.5)
    w_router_expert = nrm(ks[28], (L, D_MODEL, N_GROUPS, EXPERTS_PER_GROUP), D_MODEL ** -0.5)
    w1 = nrm(ks[29], (L, N_EXPERTS, D_MODEL, D_EXPERT), D_MODEL ** -0.5)
    w3 = nrm(ks[30], (L, N_EXPERTS, D_MODEL, D_EXPERT), D_MODEL ** -0.5)
    w2 = nrm(ks[31], (L, N_EXPERTS, D_EXPERT, D_MODEL), DEEPNORM_BETA * D_EXPERT ** -0.5)
    ln2_g = 1.0 + nrm(ks[32], (L, D_MODEL), 0.02)
    ln2_b = nrm(ks[33], (L, D_MODEL), 0.02)
    return {
        'x_prompt': x_prompt, 'x_sample': x_sample,
        'cache_k': cache_k, 'cache_v': cache_v,
        'state_ssm_re': state_ssm_re, 'state_ssm_im': state_ssm_im,
        'w_in': w_in, 'lambda_re': lambda_re, 'lambda_im': lambda_im, 'log_step': log_step,
        'b_re': b_re, 'b_im': b_im, 'c_re': c_re, 'c_im': c_im, 'd_skip': d_skip, 'w_glu': w_glu,
        'lambda_q1': lambda_q1, 'lambda_k1': lambda_k1, 'lambda_q2': lambda_q2, 'lambda_k2': lambda_k2,
        'subln_g': subln_g, 'w_attn_branch': w_attn_branch, 'w_out': w_out,
        'ln1_g': ln1_g, 'ln1_b': ln1_b,
        'w_router_group': w_router_group, 'w_router_expert': w_router_expert,
        'w1': w1, 'w3': w3, 'w2': w2, 'ln2_g': ln2_g, 'ln2_b': ln2_b,
    }


def reference(x_prompt, x_sample, cache_k, cache_v, state_ssm_re, state_ssm_im,
              w_in, lambda_re, lambda_im, log_step, b_re, b_im, c_re, c_im, d_skip, w_glu,
              lambda_q1, lambda_k1, lambda_q2, lambda_k2, subln_g, w_attn_branch, w_out,
              ln1_g, ln1_b, w_router_group, w_router_expert, w1, w3, w2, ln2_g, ln2_b):
    pos_prompt = jnp.arange(x_prompt.shape[1], dtype=jnp.int32)
    pos_sample = PAST_LEN + jnp.arange(x_sample.shape[1], dtype=jnp.int32)
    h_p, h_s = x_prompt, x_sample
    kp_l, vp_l, srp_l, sip_l, ks_l, vs_l, srs_l, sis_l = [], [], [], [], [], [], [], []
    for l in range(DEPTH):
        p = {
            'w_in': w_in[l], 'lambda_re': lambda_re[l], 'lambda_im': lambda_im[l], 'log_step': log_step[l],
            'b_re': b_re[l], 'b_im': b_im[l], 'c_re': c_re[l], 'c_im': c_im[l], 'd_skip': d_skip[l],
            'w_glu': w_glu[l], 'lambda_q1': lambda_q1[l], 'lambda_k1': lambda_k1[l],
            'lambda_q2': lambda_q2[l], 'lambda_k2': lambda_k2[l], 'subln_g': subln_g[l],
            'w_attn_branch': w_attn_branch[l], 'w_out': w_out[l], 'ln1_g': ln1_g[l], 'ln1_b': ln1_b[l],
            'w_router_group': w_router_group[l], 'w_router_expert': w_router_expert[l],
            'w1': w1[l], 'w3': w3[l], 'w2': w2[l], 'ln2_g': ln2_g[l], 'ln2_b': ln2_b[l],
        }
        lam_init = lambda_init_for(l)
        zero_state = jnp.zeros((h_p.shape[0], SSM_GROUPS, SSM_STATE), jnp.float32)
        h_p, k_p, v_p, sr_p, si_p = encoder_layer(h_p, pos_prompt, None, None, zero_state, zero_state, p, lam_init)
        h_s, k_s, v_s, sr_s, si_s = encoder_layer(h_s, pos_sample, cache_k[l], cache_v[l],
                                                  state_ssm_re[l], state_ssm_im[l], p, lam_init)
        kp_l.append(k_p.astype(cache_k.dtype))
        vp_l.append(v_p.astype(cache_v.dtype))
        srp_l.append(sr_p.astype(state_ssm_re.dtype))
        sip_l.append(si_p.astype(state_ssm_im.dtype))
        ks_l.append(k_s.astype(cache_k.dtype))
        vs_l.append(v_s.astype(cache_v.dtype))
        srs_l.append(sr_s.astype(state_ssm_re.dtype))
        sis_l.append(si_s.astype(state_ssm_im.dtype))
    return (h_p, h_s, jnp.stack(kp_l), jnp.stack(vp_l), jnp.stack(srp_l), jnp.stack(sip_l),
            jnp.stack(ks_l), jnp.stack(vs_l), jnp.stack(srs_l), jnp.stack(sis_l))
```

```python
import functools
import math

import jax
import jax.numpy as jnp
from jax import lax
from jax.experimental import pallas as pl
from jax.experimental.pallas import tpu as pltpu

F32 = jnp.float32
BF16 = jnp.bfloat16

LANES = 128
VMEM_LIMIT = 56 * 1024 * 1024

ATTN_CHUNK = 64
ROPE_THETA = 10000.0
MASK_VALUE = -1e30
SSM_MAX_RE = -1e-4
LN_EPS = 1e-5
S5_CHUNK = 32


def _cparams(sem):
    return pltpu.CompilerParams(dimension_semantics=sem, vmem_limit_bytes=VMEM_LIMIT)


def _const_spec(shape):
    nd = len(shape)
    return pl.BlockSpec(shape, lambda *_: (0,) * nd)


def _layer_norm(z, g, b):
    mu = jnp.mean(z, axis=-1, keepdims=True)
    zc = z - mu
    var = jnp.mean(zc * zc, axis=-1, keepdims=True)
    return zc * lax.rsqrt(var + LN_EPS) * g + b


def _sigmoid(x):
    return 1.0 / (1.0 + jnp.exp(-x))


def _inproj_body(x_ref, w_ref, cos_ref, sin_ref, u_ref, q_ref, kf_ref, kb_ref, vf_ref, vb_ref, g_ref,
                 *, d, q_scale):
    x = x_ref[...].astype(BF16)
    cos = cos_ref[...]
    sin = sin_ref[...]
    lane = lax.broadcasted_iota(jnp.int32, cos.shape, 1)
    first_half = (lane % 64) < 32

    def proj(j):
        return jnp.dot(x, w_ref[:, j * d:(j + 1) * d], preferred_element_type=F32)

    def rope(blk):
        outs = []
        for c in range(d // LANES):
            sub = blk[:, c * LANES:(c + 1) * LANES]
            partner = jnp.where(first_half, pltpu.roll(sub, 96, 1), pltpu.roll(sub, 32, 1))
            outs.append(sub * cos + partner * sin)
        return jnp.concatenate(outs, axis=1)

    u_ref[...] = proj(0).astype(BF16)
    q_ref[...] = (rope(proj(1)) * q_scale).astype(BF16)
    k = rope(proj(2))
    kf_ref[...] = k
    kb_ref[...] = k.astype(BF16)
    v = proj(3)
    vf_ref[...] = v
    vb_ref[...] = v.astype(BF16)
    g_ref[:, :d] = proj(4).astype(BF16)
    g_ref[:, d:] = proj(5).astype(BF16)


def _inproj(x2, w_bf, cos_t, sin_t, *, tm, q_scale):
    n, d = x2.shape
    width = w_bf.shape[1]
    n_tab = cos_t.shape[0] // tm
    row = lambda i: (i, 0)
    tab = lambda i: (i % n_tab, 0)
    outs = (
        jax.ShapeDtypeStruct((n, d), BF16),
        jax.ShapeDtypeStruct((n, d), BF16),
        jax.ShapeDtypeStruct((n, d), F32),
        jax.ShapeDtypeStruct((n, d), BF16),
        jax.ShapeDtypeStruct((n, d), F32),
        jax.ShapeDtypeStruct((n, d), BF16),
        jax.ShapeDtypeStruct((n, 2 * d), BF16),
    )
    return pl.pallas_call(
        functools.partial(_inproj_body, d=d, q_scale=q_scale),
        grid=(n // tm,),
        in_specs=[
            pl.BlockSpec((tm, d), row),
            pl.BlockSpec((d, width), lambda i: (0, 0), pipeline_mode=pl.Buffered(1)),
            pl.BlockSpec((tm, LANES), tab),
            pl.BlockSpec((tm, LANES), tab),
        ],
        out_specs=(
            pl.BlockSpec((tm, d), row), pl.BlockSpec((tm, d), row), pl.BlockSpec((tm, d), row),
            pl.BlockSpec((tm, d), row), pl.BlockSpec((tm, d), row), pl.BlockSpec((tm, d), row),
            pl.BlockSpec((tm, 2 * d), row),
        ),
        out_shape=outs,
        compiler_params=_cparams(("parallel",)),
        name="inproj_rope",
    )(x2, w_bf, cos_t, sin_t)


def _s5_state_body(u_ref, w_ref, zre_ref, zim_ref):
    u = jnp.concatenate([u_ref[0], u_ref[1]], axis=1)
    z = jnp.dot(u, w_ref[...], preferred_element_type=F32)
    zre_ref[...] = z[:, :LANES]
    zim_ref[...] = z[:, LANES:]


def _s5_state(u_g, w_state):
    gp, _, r, cw = u_g.shape
    return pl.pallas_call(
        _s5_state_body,
        grid=(gp,),
        in_specs=[
            pl.BlockSpec((None, 2, r, cw), lambda g: (g, 0, 0, 0)),
            pl.BlockSpec((None, 2 * cw, 2 * LANES), lambda g: (g, 0, 0)),
        ],
        out_specs=(pl.BlockSpec((r, LANES), lambda g: (0, g)), pl.BlockSpec((r, LANES), lambda g: (0, g))),
        out_shape=(jax.ShapeDtypeStruct((r, gp * LANES), F32), jax.ShapeDtypeStruct((r, gp * LANES), F32)),
        compiler_params=_cparams(("parallel",)),
        name="s5_chunk_state",
    )(u_g, w_state)


def _s5_scan_body(zre_ref, zim_ref, are_ref, aim_ref, s0re_ref, s0im_ref,
                  sre_ref, sim_ref, fre_ref, fim_ref, *, n_c):
    ar = are_ref[...]
    ai = aim_ref[...]

    def step(c, carry):
        s_re, s_im = carry
        sre_ref[c] = s_re
        sim_ref[c] = s_im
        n_re = ar * s_re - ai * s_im + zre_ref[c]
        n_im = ar * s_im + ai * s_re + zim_ref[c]
        return n_re, n_im

    f_re, f_im = lax.fori_loop(0, n_c, step, (s0re_ref[...], s0im_ref[...]))
    fre_ref[...] = f_re
    fim_ref[...] = f_im


def _s5_scan(z_re, z_im, a_re, a_im, s0_re, s0_im, *, lane_blk=256):
    n_c, nb, w = z_re.shape
    seq = pl.BlockSpec((n_c, nb, lane_blk), lambda j: (0, 0, j))
    vec = pl.BlockSpec((1, lane_blk), lambda j: (0, j))
    st = pl.BlockSpec((nb, lane_blk), lambda j: (0, j))
    return pl.pallas_call(
        functools.partial(_s5_scan_body, n_c=n_c),
        grid=(w // lane_blk,),
        in_specs=[seq, seq, vec, vec, st, st],
        out_specs=(seq, seq, st, st),
        out_shape=(jax.ShapeDtypeStruct((n_c, nb, w), F32), jax.ShapeDtypeStruct((n_c, nb, w), F32),
                   jax.ShapeDtypeStruct((nb, w), F32), jax.ShapeDtypeStruct((nb, w), F32)),
        compiler_params=_cparams(("parallel",)),
        name="s5_chunk_scan",
    )(z_re, z_im, a_re, a_im, s0_re, s0_im)


def _s5_out_body(u_ref, mt_ref, sre_ref, sim_ref, vre_ref, vim_ref, y_ref):
    s_re = sre_ref[...].astype(BF16)
    s_im = sim_ref[...].astype(BF16)
    inter = (jnp.dot(s_re, vre_ref[...], preferred_element_type=F32)
             + jnp.dot(s_im, vim_ref[...], preferred_element_type=F32))
    cw = u_ref.shape[-1]
    for i in range(2):
        intra = jnp.dot(u_ref[i], mt_ref[i], preferred_element_type=F32)
        y_ref[i] = (intra + inter[:, i * cw:(i + 1) * cw]).astype(BF16)


def _s5_out(u_g, mt, s_re, s_im, v_re, v_im):
    gp, _, r, cw = u_g.shape
    return pl.pallas_call(
        _s5_out_body,
        grid=(gp,),
        in_specs=[
            pl.BlockSpec((None, 2, r, cw), lambda g: (g, 0, 0, 0)),
            pl.BlockSpec((None, 2, cw, cw), lambda g: (g, 0, 0, 0)),
            pl.BlockSpec((r, LANES), lambda g: (0, g)),
            pl.BlockSpec((r, LANES), lambda g: (0, g)),
            pl.BlockSpec((None, LANES, 2 * cw), lambda g: (g, 0, 0)),
            pl.BlockSpec((None, LANES, 2 * cw), lambda g: (g, 0, 0)),
        ],
        out_specs=pl.BlockSpec((None, 2, r, cw), lambda g: (g, 0, 0, 0)),
        out_shape=jax.ShapeDtypeStruct((gp, 2, r, cw), BF16),
        compiler_params=_cparams(("parallel",)),
        name="s5_toeplitz_out",
    )(u_g, mt, s_re, s_im, v_re, v_im)


def _s5_operators(lambda_re, lambda_im, log_step, b_re, b_im, c_re, c_im, d_skip, chunk):
    hp = lax.Precision.HIGHEST
    g, p, c = b_re.shape
    lam_re = jnp.minimum(lambda_re.astype(F32), SSM_MAX_RE)
    lam_im = lambda_im.astype(F32)
    step = jnp.exp(log_step.astype(F32))[:, None]
    dd = jnp.arange(chunk + 1, dtype=F32)
    mag = jnp.exp((lam_re * step)[..., None] * dd)
    ang = (lam_im * step)[..., None] * dd
    pw_re = mag * jnp.cos(ang)
    pw_im = mag * jnp.sin(ang)
    a_re, a_im = pw_re[..., 1], pw_im[..., 1]
    den = lam_re * lam_re + lam_im * lam_im
    k_re = ((a_re - 1.0) * lam_re + a_im * lam_im) / den
    k_im = (a_im * lam_re - (a_re - 1.0) * lam_im) / den
    bb_re = k_re[..., None] * b_re - k_im[..., None] * b_im
    bb_im = k_re[..., None] * b_im + k_im[..., None] * b_re
    e_re = c_re[:, None] * jnp.moveaxis(pw_re, 2, 1)[:, :, None, :] - c_im[:, None] * jnp.moveaxis(pw_im, 2, 1)[:, :, None, :]
    e_im = c_re[:, None] * jnp.moveaxis(pw_im, 2, 1)[:, :, None, :] + c_im[:, None] * jnp.moveaxis(pw_re, 2, 1)[:, :, None, :]
    kern = (jnp.einsum('gdcp,gpk->gdck', e_re[:, :chunk], bb_re, precision=hp)
            - jnp.einsum('gdcp,gpk->gdck', e_im[:, :chunk], bb_im, precision=hp))
    kern = kern.at[:, 0].add(d_skip.astype(F32)[:, :, None] * jnp.eye(c, dtype=F32))
    idx = jnp.arange(chunk)
    lag = idx[None, :] - idx[:, None]
    kt = jnp.swapaxes(kern, 2, 3)
    mt = jnp.where((lag >= 0)[None, :, :, None, None], kt[:, jnp.clip(lag, 0, chunk - 1)], 0.0)
    mt = jnp.transpose(mt, (0, 1, 3, 2, 4)).reshape(g, chunk * c, chunk * c)
    rv_re = jnp.moveaxis(pw_re[..., :chunk][..., ::-1], 2, 1)
    rv_im = jnp.moveaxis(pw_im[..., :chunk][..., ::-1], 2, 1)
    bt_re = jnp.swapaxes(bb_re, 1, 2)
    bt_im = jnp.swapaxes(bb_im, 1, 2)
    w_re = (rv_re[:, :, None] * bt_re[:, None] - rv_im[:, :, None] * bt_im[:, None]).reshape(g, chunk * c, p)
    w_im = (rv_re[:, :, None] * bt_im[:, None] + rv_im[:, :, None] * bt_re[:, None]).reshape(g, chunk * c, p)
    v_re = jnp.transpose(e_re[:, 1:], (0, 3, 1, 2)).reshape(g, p, chunk * c)
    v_im = -jnp.transpose(e_im[:, 1:], (0, 3, 1, 2)).reshape(g, p, chunk * c)
    gp, cw = g // 2, chunk * c
    z = jnp.zeros((gp, cw, p), F32)
    wr = w_re.reshape(gp, 2, cw, p)
    wi = w_im.reshape(gp, 2, cw, p)
    w_state = jnp.concatenate([
        jnp.concatenate([wr[:, 0], z, wi[:, 0], z], axis=2),
        jnp.concatenate([z, wr[:, 1], z, wi[:, 1]], axis=2)], axis=1)
    zv = jnp.zeros((gp, p, cw), F32)
    vr = v_re.reshape(gp, 2, p, cw)
    vi = v_im.reshape(gp, 2, p, cw)
    v_re_pair = jnp.concatenate([jnp.concatenate([vr[:, 0], zv], axis=2),
                                 jnp.concatenate([zv, vr[:, 1]], axis=2)], axis=1)
    v_im_pair = jnp.concatenate([jnp.concatenate([vi[:, 0], zv], axis=2),
                                 jnp.concatenate([zv, vi[:, 1]], axis=2)], axis=1)
    al_re = pw_re[..., chunk].reshape(1, g * p)
    al_im = pw_im[..., chunk].reshape(1, g * p)
    return dict(mt=mt.reshape(gp, 2, cw, cw).astype(BF16), w_state=w_state.astype(BF16),
                v_re=v_re_pair.astype(BF16), v_im=v_im_pair.astype(BF16), al_re=al_re, al_im=al_im)


def _s5_branch(u2, s0_re, s0_im, ops, *, nb, t, g, c, chunk):
    n_c = t // chunk
    gp, cw = g // 2, chunk * c
    r = n_c * nb
    u_g = u2.reshape(nb, n_c, chunk, gp, 2, c)
    u_g = jnp.transpose(u_g, (3, 4, 1, 0, 2, 5)).reshape(gp, 2, r, cw)
    z_re, z_im = _s5_state(u_g, ops['w_state'])
    p = s0_re.shape[-1]
    w = g * p
    s_re, s_im, f_re, f_im = _s5_scan(z_re.reshape(n_c, nb, w), z_im.reshape(n_c, nb, w),
                                      ops['al_re'], ops['al_im'], s0_re.reshape(nb, w), s0_im.reshape(nb, w))
    y_g = _s5_out(u_g, ops['mt'], s_re.reshape(r, w), s_im.reshape(r, w), ops['v_re'], ops['v_im'])
    y = jnp.transpose(y_g.reshape(gp, 2, n_c, nb, chunk, c), (3, 2, 4, 0, 1, 5)).reshape(nb * t, g * c)
    return y, f_re.reshape(nb, g, p), f_im.reshape(nb, g, p)


def _split_heads_q(q):
    lane = lax.broadcasted_iota(jnp.int32, q.shape, 1)
    zero = jnp.zeros_like(q)
    return jnp.concatenate([jnp.where(lane < 64, q, zero), jnp.where(lane >= 64, q, zero)], axis=0)


def _diff_finish(acc, l, lam, g, out_scale, tq):
    o = acc[:tq] / l[:tq] - lam * (acc[tq:] / l[tq:])
    ms = jnp.mean(o * o, axis=-1, keepdims=True)
    return o * lax.rsqrt(ms + LN_EPS) * g * out_scale


def _attn_prompt_body(lam_ref, q_ref, k_ref, v_ref, g_ref, o_ref, m_sc, l_sc, acc_sc, *, tq, out_scale):
    qi = pl.program_id(2)
    qz = _split_heads_q(q_ref[...])
    m_sc[...] = jnp.full(m_sc.shape, MASK_VALUE, F32)
    l_sc[...] = jnp.zeros(l_sc.shape, F32)
    acc_sc[...] = jnp.zeros(acc_sc.shape, F32)

    def step(kb, masked):
        off = pl.multiple_of(kb * tq, tq)
        k = k_ref[pl.ds(off, tq), :]
        v = v_ref[pl.ds(off, tq), :]
        s = lax.dot_general(qz, k, (((1,), (1,)), ((), ())), preferred_element_type=F32)
        if masked:
            row = lax.broadcasted_iota(jnp.int32, s.shape, 0)
            col = lax.broadcasted_iota(jnp.int32, s.shape, 1)
            s = jnp.where((col // ATTN_CHUNK) <= ((row % tq) // ATTN_CHUNK), s, MASK_VALUE)
        m_prev = m_sc[...]
        m_new = jnp.maximum(m_prev, jnp.max(s, axis=1, keepdims=True))
        alpha = jnp.exp(m_prev - m_new)
        p = jnp.exp(s - m_new)
        l_sc[...] = alpha * l_sc[...] + jnp.sum(p, axis=1, keepdims=True)
        acc_sc[...] = alpha * acc_sc[...] + jnp.dot(p.astype(BF16), v, preferred_element_type=F32)
        m_sc[...] = m_new

    def full_step(kb, carry):
        step(kb, False)
        return carry

    lax.fori_loop(0, qi, full_step, 0)
    step(qi, True)
    o_ref[...] = _diff_finish(acc_sc[...], l_sc[...], lam_ref[0], g_ref[...], out_scale, tq).astype(BF16)


def _attn_prompt(lam, q3, k3, v3, g_row, *, heads, tq, out_scale):
    b, t, d = q3.shape
    hw = d // heads
    return pl.pallas_call(
        functools.partial(_attn_prompt_body, tq=tq, out_scale=out_scale),
        grid=(b, heads, t // tq),
        in_specs=[
            pl.BlockSpec(memory_space=pltpu.SMEM),
            pl.BlockSpec((None, tq, hw), lambda bi, h, qi: (bi, qi, h)),
            pl.BlockSpec((None, t, hw), lambda bi, h, qi: (bi, 0, h)),
            pl.BlockSpec((None, t, hw), lambda bi, h, qi: (bi, 0, h)),
            pl.BlockSpec((1, hw), lambda bi, h, qi: (0, 0)),
        ],
        out_specs=pl.BlockSpec((None, tq, hw), lambda bi, h, qi: (bi, qi, h)),
        out_shape=jax.ShapeDtypeStruct((b, t, d), BF16),
        scratch_shapes=[pltpu.VMEM((2 * tq, 1), F32), pltpu.VMEM((2 * tq, 1), F32), pltpu.VMEM((2 * tq, hw), F32)],
        compiler_params=_cparams(("parallel", "parallel", "arbitrary")),
        name="diff_attn_prompt",
    )(lam, q3, k3, v3, g_row)


def _attn_sample_body(lam_ref, q_ref, kc_ref, kn_ref, vc_ref, vn_ref, g_ref, o_ref, *, heads, out_scale):
    tq = q_ref.shape[0]
    hw = q_ref.shape[1] // heads
    dn = (((1,), (1,)), ((), ()))
    for h in range(heads):
        sl = slice(h * hw, (h + 1) * hw)
        qz = _split_heads_q(q_ref[:, sl])
        s_c = lax.dot_general(qz, kc_ref[:, sl].astype(BF16), dn, preferred_element_type=F32)
        s_n = lax.dot_general(qz, kn_ref[:, sl], dn, preferred_element_type=F32)
        m = jnp.maximum(jnp.max(s_c, axis=1, keepdims=True), jnp.max(s_n, axis=1, keepdims=True))
        p_c = jnp.exp(s_c - m)
        p_n = jnp.exp(s_n - m)
        l = jnp.sum(p_c, axis=1, keepdims=True) + jnp.sum(p_n, axis=1, keepdims=True)
        acc = (jnp.dot(p_c.astype(BF16), vc_ref[:, sl].astype(BF16), preferred_element_type=F32)
               + jnp.dot(p_n.astype(BF16), vn_ref[:, sl], preferred_element_type=F32))
        o_ref[:, sl] = _diff_finish(acc, l, lam_ref[0], g_ref[...], out_scale, tq).astype(BF16)


def _attn_sample(lam, q3, kc3, kn3, vc3, vn3, g_row, *, heads, out_scale):
    b, t, d = q3.shape
    past = kc3.shape[1]
    new = pl.BlockSpec((None, t, d), lambda bi: (bi, 0, 0))
    old = pl.BlockSpec((None, past, d), lambda bi: (bi, 0, 0))
    return pl.pallas_call(
        functools.partial(_attn_sample_body, heads=heads, out_scale=out_scale),
        grid=(b,),
        in_specs=[pl.BlockSpec(memory_space=pltpu.SMEM), new, old, new, old, new,
                  pl.BlockSpec((1, d // heads), lambda bi: (0, 0))],
        out_specs=new,
        out_shape=jax.ShapeDtypeStruct((b, t, d), BF16),
        compiler_params=_cparams(("parallel",)),
        name="diff_attn_sample",
    )(lam, q3, kc3, kn3, vc3, vn3, g_row)


def _route(logits, n_groups, epg):
    n_exp = n_groups * epg
    lane = lax.broadcasted_iota(jnp.int32, logits.shape, 1)
    neg = jnp.float32(-jnp.inf)
    is_g = (lane >= n_exp) & (lane < n_exp + n_groups)
    gl = jnp.where(is_g, logits, neg)
    gmax = jnp.max(gl, axis=1, keepdims=True)
    g_idx = jnp.min(jnp.where(gl == gmax, lane - n_exp, n_groups), axis=1, keepdims=True)
    g_w = 1.0 / jnp.sum(jnp.where(is_g, jnp.exp(gl - gmax), 0.0), axis=1, keepdims=True)
    in_grp = (lane < n_exp) & ((lane // epg) == g_idx)
    el = jnp.where(in_grp, logits, neg)
    t1 = jnp.max(el, axis=1, keepdims=True)
    i1 = jnp.min(jnp.where(el == t1, lane, LANES), axis=1, keepdims=True)
    el2 = jnp.where(lane == i1, neg, el)
    t2 = jnp.max(el2, axis=1, keepdims=True)
    i2 = jnp.min(jnp.where(el2 == t2, lane, LANES), axis=1, keepdims=True)
    e2 = jnp.exp(t2 - t1)
    w1 = g_w / (1.0 + e2)
    w2 = g_w * e2 / (1.0 + e2)
    return jnp.where(lane == i1, w1, 0.0) + jnp.where(lane == i2, w2, 0.0)


def _post_body(attn_ref, y_ref, gates_ref, x_ref, wa_ref, wglu_ref, wout_ref, lng_ref, lnb_ref,
               wrh_ref, wrl_ref, h_ref, hb_ref, route_ref, *, d, alpha, n_groups, epg):
    b_attn = jnp.dot(attn_ref[...], wa_ref[...], preferred_element_type=F32)
    yg = jax.nn.gelu(y_ref[...].astype(F32)).astype(BF16)
    glu = jnp.dot(yg, wglu_ref[...], preferred_element_type=F32)
    b_ssm = glu[:, :d] * _sigmoid(glu[:, d:])
    merged = (_sigmoid(gates_ref[:, :d].astype(F32)) * b_ssm
              + _sigmoid(gates_ref[:, d:].astype(F32)) * b_attn)
    mix = jnp.dot(merged.astype(BF16), wout_ref[...], preferred_element_type=F32)
    h = _layer_norm(alpha * x_ref[...] + mix, lng_ref[...], lnb_ref[...])
    h_ref[...] = h
    h_hi = h.astype(BF16)
    hb_ref[...] = h_hi
    h_lo = (h - h_hi.astype(F32)).astype(BF16)
    logits = (jnp.dot(h_hi, wrh_ref[...], preferred_element_type=F32)
              + jnp.dot(h_lo, wrh_ref[...], preferred_element_type=F32)
              + jnp.dot(h_hi, wrl_ref[...], preferred_element_type=F32))
    route_ref[...] = _route(logits, n_groups, epg)


def _post(attn2, y2, gates2, x2, wa, wglu, wout, lng, lnb, wr_hi, wr_lo, *, tm, alpha, n_groups, epg):
    n, d = x2.shape
    row = lambda i: (i, 0)
    wspec = lambda shape: pl.BlockSpec(shape, lambda i: (0, 0), pipeline_mode=pl.Buffered(1))
    return pl.pallas_call(
        functools.partial(_post_body, d=d, alpha=alpha, n_groups=n_groups, epg=epg),
        grid=(n // tm,),
        in_specs=[
            pl.BlockSpec((tm, d), row), pl.BlockSpec((tm, d), row), pl.BlockSpec((tm, 2 * d), row),
            pl.BlockSpec((tm, d), row),
            wspec((d, d)), wspec((d, 2 * d)), wspec((d, d)),
            _const_spec((1, d)), _const_spec((1, d)),
            wspec((d, LANES)), wspec((d, LANES)),
        ],
        out_specs=(pl.BlockSpec((tm, d), row), pl.BlockSpec((tm, d), row), pl.BlockSpec((tm, LANES), row)),
        out_shape=(jax.ShapeDtypeStruct((n, d), F32), jax.ShapeDtypeStruct((n, d), BF16),
                   jax.ShapeDtypeStruct((n, LANES), F32)),
        compiler_params=_cparams(("parallel",)),
        name="merge_outproj_ln_router",
    )(attn2, y2, gates2, x2, wa, wglu, wout, lng, lnb, wr_hi, wr_lo)


def _moe_body(hb_ref, h_ref, route_ref, w1_ref, w3_ref, w2_ref, lng_ref, lnb_ref, o_ref, acc_sc, *, alpha):
    e = pl.program_id(1)

    @pl.when(e == 0)
    def _():
        acc_sc[...] = jnp.zeros(acc_sc.shape, F32)

    x = hb_ref[...]
    a = jnp.dot(x, w1_ref[...], preferred_element_type=F32)
    b = jnp.dot(x, w3_ref[...], preferred_element_type=F32)
    route = route_ref[...]
    lane = lax.broadcasted_iota(jnp.int32, route.shape, 1)
    gate = jnp.sum(jnp.where(lane == e, route, 0.0), axis=1, keepdims=True)
    mid = (a * _sigmoid(a) * b * gate).astype(BF16)
    acc_sc[...] += jnp.dot(mid, w2_ref[...], preferred_element_type=F32)

    @pl.when(e == pl.num_programs(1) - 1)
    def _():
        o_ref[...] = _layer_norm(alpha * h_ref[...] + acc_sc[...], lng_ref[...], lnb_ref[...])


def _moe(hb, h, route, w1, w3, w2, lng, lnb, *, tm, alpha):
    n, d = h.shape
    n_exp, _, de = w1.shape
    row = lambda i, e: (i, 0)
    return pl.pallas_call(
        functools.partial(_moe_body, alpha=alpha),
        grid=(n // tm, n_exp),
        in_specs=[
            pl.BlockSpec((tm, d), row), pl.BlockSpec((tm, d), row), pl.BlockSpec((tm, LANES), row),
            pl.BlockSpec((None, d, de), lambda i, e: (e, 0, 0)),
            pl.BlockSpec((None, d, de), lambda i, e: (e, 0, 0)),
            pl.BlockSpec((None, de, d), lambda i, e: (e, 0, 0)),
            pl.BlockSpec((1, d), lambda i, e: (0, 0)), pl.BlockSpec((1, d), lambda i, e: (0, 0)),
        ],
        out_specs=pl.BlockSpec((tm, d), row),
        out_shape=jax.ShapeDtypeStruct((n, d), F32),
        scratch_shapes=[pltpu.VMEM((tm, d), F32)],
        compiler_params=_cparams(("parallel", "arbitrary")),
        name="moe_experts_ln",
    )(hb, h, route, w1, w3, w2, lng, lnb)


def _rope_tables(pos, rows):
    half = 32
    inv_freq = ROPE_THETA ** (-jnp.arange(half, dtype=F32) / half)
    ang = pos.astype(F32)[:, None] * inv_freq[None, :]
    cos = jnp.cos(ang)
    sin = jnp.sin(ang)
    cos_t = jnp.tile(jnp.concatenate([cos, cos], axis=1), (1, LANES // 64))
    sin_t = jnp.tile(jnp.concatenate([-sin, sin], axis=1), (1, LANES // 64))
    reps = max(1, rows // cos_t.shape[0])
    return jnp.tile(cos_t, (reps, 1)), jnp.tile(sin_t, (reps, 1))


def _encoder_layer(x, pos, k_past, v_past, s_re, s_im, p, lam_init, alpha):
    nb, t, d = x.shape
    n = nb * t
    heads = p['heads']
    g, pstate = p['groups'], p['pstate']
    tm = min(512, n)
    x2 = x.reshape(n, d)
    cos_t, sin_t = _rope_tables(pos, tm)
    head_dim = d // heads // 2
    u, q, kf, kb, vf, vb, gates = _inproj(x2, p['w_in'], cos_t, sin_t, tm=tm, q_scale=head_dim ** -0.5)

    out_scale = 1.0 - lam_init
    if k_past is None:
        attn = _attn_prompt(p['lam'], q.reshape(nb, t, d), kb.reshape(nb, t, d), vb.reshape(nb, t, d),
                            p['subln_g'], heads=heads, tq=min(256, t), out_scale=out_scale)
    else:
        past = k_past.shape[1]
        attn = _attn_sample(p['lam'], q.reshape(nb, t, d), k_past.reshape(nb, past, d), kb.reshape(nb, t, d),
                            v_past.reshape(nb, past, d), vb.reshape(nb, t, d), p['subln_g'],
                            heads=heads, out_scale=out_scale)

    y, f_re, f_im = _s5_branch(u, s_re, s_im, p['s5'], nb=nb, t=t, g=g, c=d // g, chunk=S5_CHUNK)

    h, hb, route = _post(attn.reshape(n, d), y, gates, x2, p['w_attn_branch'], p['w_glu'], p['w_out'],
                         p['ln1_g'], p['ln1_b'], p['wr_hi'], p['wr_lo'],
                         tm=tm, alpha=alpha, n_groups=p['n_groups'], epg=p['epg'])
    out = _moe(hb, h, route, p['w1'], p['w3'], p['w2'], p['ln2_g'], p['ln2_b'], tm=min(1024, n), alpha=alpha)
    k_new = kf.reshape(nb, t, heads, 2, head_dim)
    v_new = vf.reshape(nb, t, heads, 2 * head_dim)
    return out.reshape(nb, t, d), k_new, v_new, f_re, f_im


def kernel(x_prompt, x_sample, cache_k, cache_v, state_ssm_re, state_ssm_im, w_in, lambda_re, lambda_im, log_step, b_re, b_im, c_re, c_im, d_skip, w_glu, lambda_q1, lambda_k1, lambda_q2, lambda_k2, subln_g, w_attn_branch, w_out, ln1_g, ln1_b, w_router_group, w_router_expert, w1, w3, w2, ln2_g, ln2_b):
    depth = w_in.shape[0]
    d = x_prompt.shape[-1]
    heads = cache_k.shape[3]
    past_len = cache_k.shape[2]
    groups, pstate = lambda_re.shape[1], lambda_re.shape[2]
    n_groups, epg = w_router_expert.shape[2], w_router_expert.shape[3]
    n_exp = n_groups * epg
    alpha = (2.0 * depth) ** 0.25
    pos_prompt = jnp.arange(x_prompt.shape[1], dtype=jnp.int32)
    pos_sample = past_len + jnp.arange(x_sample.shape[1], dtype=jnp.int32)

    h_p, h_s = x_prompt, x_sample
    outs = [[] for _ in range(8)]
    for l in range(depth):
        lam_init = 0.8 - 0.6 * math.exp(-0.3 * l)
        lam = (jnp.exp(jnp.sum(lambda_q1[l].astype(F32) * lambda_k1[l].astype(F32)))
               - jnp.exp(jnp.sum(lambda_q2[l].astype(F32) * lambda_k2[l].astype(F32))) + lam_init)
        wr = jnp.concatenate([w_router_expert[l].reshape(d, n_exp).astype(F32), w_router_group[l].astype(F32),
                              jnp.zeros((d, LANES - n_exp - n_groups), F32)], axis=1)
        wr_hi = wr.astype(BF16)
        p = dict(
            heads=heads, groups=groups, pstate=pstate, n_groups=n_groups, epg=epg,
            w_in=w_in[l].astype(BF16), lam=lam.reshape(1).astype(F32),
            subln_g=subln_g[l].astype(F32).reshape(1, -1),
            s5=_s5_operators(lambda_re[l], lambda_im[l], log_step[l], b_re[l].astype(F32), b_im[l].astype(F32),
                             c_re[l].astype(F32), c_im[l].astype(F32), d_skip[l], S5_CHUNK),
            w_attn_branch=w_attn_branch[l].astype(BF16), w_glu=w_glu[l].astype(BF16), w_out=w_out[l].astype(BF16),
            ln1_g=ln1_g[l].astype(F32).reshape(1, d), ln1_b=ln1_b[l].astype(F32).reshape(1, d),
            ln2_g=ln2_g[l].astype(F32).reshape(1, d), ln2_b=ln2_b[l].astype(F32).reshape(1, d),
            wr_hi=wr_hi, wr_lo=(wr - wr_hi.astype(F32)).astype(BF16),
            w1=w1[l].astype(BF16), w3=w3[l].astype(BF16), w2=w2[l].astype(BF16),
        )
        zero_state = jnp.zeros((h_p.shape[0], groups, pstate), F32)
        h_p, k_p, v_p, sr_p, si_p = _encoder_layer(h_p, pos_prompt, None, None, zero_state, zero_state, p, lam_init, alpha)
        h_s, k_s, v_s, sr_s, si_s = _encoder_layer(h_s, pos_sample, cache_k[l], cache_v[l],
                                                   state_ssm_re[l].astype(F32), state_ssm_im[l].astype(F32),
                                                   p, lam_init, alpha)
        for lst, val in zip(outs, (k_p.astype(cache_k.dtype), v_p.astype(cache_v.dtype),
                                   sr_p.astype(state_ssm_re.dtype), si_p.astype(state_ssm_im.dtype),
                                   k_s.astype(cache_k.dtype), v_s.astype(cache_v.dtype),
                                   sr_s.astype(state_ssm_re.dtype), si_s.astype(state_ssm_im.dtype))):
            lst.append(val)
    return (h_p, h_s) + tuple(jnp.stack(o) for o in outs)
```

```python
import functools
import math

import jax
import jax.numpy as jnp
from jax import lax
from jax.experimental import pallas as pl
from jax.experimental.pallas import tpu as pltpu

F32 = jnp.float32
BF16 = jnp.bfloat16

LANES = 128
VMEM_LIMIT = 56 * 1024 * 1024

ATTN_CHUNK = 64
ROPE_THETA = 10000.0
MASK_VALUE = -1e30
SSM_MAX_RE = -1e-4
LN_EPS = 1e-5
S5_CHUNK = 32


def _cparams(sem):
    return pltpu.CompilerParams(dimension_semantics=sem, vmem_limit_bytes=VMEM_LIMIT)


def _const_spec(shape):
    nd = len(shape)
    return pl.BlockSpec(shape, lambda *_: (0,) * nd)


def _layer_norm(z, g, b):
    mu = jnp.mean(z, axis=-1, keepdims=True)
    zc = z - mu
    var = jnp.mean(zc * zc, axis=-1, keepdims=True)
    return zc * lax.rsqrt(var + LN_EPS) * g + b


def _sigmoid(x):
    return 1.0 / (1.0 + jnp.exp(-x))


def _inproj_body(x_ref, w_ref, cos_ref, sin_ref, u_ref, q_ref, kf_ref, kb_ref, vf_ref, vb_ref, g_ref,
                 *, d, q_scale):
    x = x_ref[...].astype(BF16)
    cos = cos_ref[...]
    sin = sin_ref[...]
    lane = lax.broadcasted_iota(jnp.int32, cos.shape, 1)
    first_half = (lane % 64) < 32

    def proj(j):
        return jnp.dot(x, w_ref[:, j * d:(j + 1) * d], preferred_element_type=F32)

    def rope(blk):
        outs = []
        for c in range(d // LANES):
            sub = blk[:, c * LANES:(c + 1) * LANES]
            partner = jnp.where(first_half, pltpu.roll(sub, 96, 1), pltpu.roll(sub, 32, 1))
            outs.append(sub * cos + partner * sin)
        return jnp.concatenate(outs, axis=1)

    u_ref[...] = proj(0).astype(BF16)
    q_ref[...] = (rope(proj(1)) * q_scale).astype(BF16)
    k = rope(proj(2))
    kf_ref[...] = k
    kb_ref[...] = k.astype(BF16)
    v = proj(3)
    vf_ref[...] = v
    vb_ref[...] = v.astype(BF16)
    g_ref[:, :d] = proj(4).astype(BF16)
    g_ref[:, d:] = proj(5).astype(BF16)


def _inproj(x2, w_bf, cos_t, sin_t, *, tm, q_scale):
    n, d = x2.shape
    width = w_bf.shape[1]
    n_tab = cos_t.shape[0] // tm
    row = lambda i: (i, 0)
    tab = lambda i: (i % n_tab, 0)
    outs = (
        jax.ShapeDtypeStruct((n, d), BF16),
        jax.ShapeDtypeStruct((n, d), BF16),
        jax.ShapeDtypeStruct((n, d), F32),
        jax.ShapeDtypeStruct((n, d), BF16),
        jax.ShapeDtypeStruct((n, d), F32),
        jax.ShapeDtypeStruct((n, d), BF16),
        jax.ShapeDtypeStruct((n, 2 * d), BF16),
    )
    return pl.pallas_call(
        functools.partial(_inproj_body, d=d, q_scale=q_scale),
        grid=(n // tm,),
        in_specs=[
            pl.BlockSpec((tm, d), row),
            pl.BlockSpec((d, width), lambda i: (0, 0), pipeline_mode=pl.Buffered(1)),
            pl.BlockSpec((tm, LANES), tab),
            pl.BlockSpec((tm, LANES), tab),
        ],
        out_specs=(
            pl.BlockSpec((tm, d), row), pl.BlockSpec((tm, d), row), pl.BlockSpec((tm, d), row),
            pl.BlockSpec((tm, d), row), pl.BlockSpec((tm, d), row), pl.BlockSpec((tm, d), row),
            pl.BlockSpec((tm, 2 * d), row),
        ),
        out_shape=outs,
        compiler_params=_cparams(("parallel",)),
        name="inproj_rope",
    )(x2, w_bf, cos_t, sin_t)


def _s5_state_body(u_ref, w_ref, zre_ref, zim_ref):
    u = jnp.concatenate([u_ref[0], u_ref[1]], axis=1)
    z = jnp.dot(u, w_ref[...], preferred_element_type=F32)
    zre_ref[...] = z[:, :LANES]
    zim_ref[...] = z[:, LANES:]


def _s5_state(u_g, w_state):
    gp, _, r, cw = u_g.shape
    return pl.pallas_call(
        _s5_state_body,
        grid=(gp,),
        in_specs=[
            pl.BlockSpec((None, 2, r, cw), lambda g: (g, 0, 0, 0)),
            pl.BlockSpec((None, 2 * cw, 2 * LANES), lambda g: (g, 0, 0)),
        ],
        out_specs=(pl.BlockSpec((r, LANES), lambda g: (0, g)), pl.BlockSpec((r, LANES), lambda g: (0, g))),
        out_shape=(jax.ShapeDtypeStruct((r, gp * LANES), F32), jax.ShapeDtypeStruct((r, gp * LANES), F32)),
        compiler_params=_cparams(("parallel",)),
        name="s5_chunk_state",
    )(u_g, w_state)


def _s5_scan_body(zre_ref, zim_ref, are_ref, aim_ref, s0re_ref, s0im_ref,
                  sre_ref, sim_ref, fre_ref, fim_ref, *, n_c):
    ar = are_ref[...]
    ai = aim_ref[...]

    def step(c, carry):
        s_re, s_im = carry
        sre_ref[c] = s_re
        sim_ref[c] = s_im
        n_re = ar * s_re - ai * s_im + zre_ref[c]
        n_im = ar * s_im + ai * s_re + zim_ref[c]
        return n_re, n_im

    f_re, f_im = lax.fori_loop(0, n_c, step, (s0re_ref[...], s0im_ref[...]))
    fre_ref[...] = f_re
    fim_ref[...] = f_im


def _s5_scan(z_re, z_im, a_re, a_im, s0_re, s0_im, *, lane_blk=256):
    n_c, nb, w = z_re.shape
    seq = pl.BlockSpec((n_c, nb, lane_blk), lambda j: (0, 0, j))
    vec = pl.BlockSpec((1, lane_blk), lambda j: (0, j))
    st = pl.BlockSpec((nb, lane_blk), lambda j: (0, j))
    return pl.pallas_call(
        functools.partial(_s5_scan_body, n_c=n_c),
        grid=(w // lane_blk,),
        in_specs=[seq, seq, vec, vec, st, st],
        out_specs=(seq, seq, st, st),
        out_shape=(jax.ShapeDtypeStruct((n_c, nb, w), F32), jax.ShapeDtypeStruct((n_c, nb, w), F32),
                   jax.ShapeDtypeStruct((nb, w), F32), jax.ShapeDtypeStruct((nb, w), F32)),
        compiler_params=_cparams(("parallel",)),
        name="s5_chunk_scan",
    )(z_re, z_im, a_re, a_im, s0_re, s0_im)


def _s5_out_body(u_ref, mt_ref, sre_ref, sim_ref, vre_ref, vim_ref, y_ref):
    s_re = sre_ref[...].astype(BF16)
    s_im = sim_ref[...].astype(BF16)
    inter = (jnp.dot(s_re, vre_ref[...], preferred_element_type=F32)
             + jnp.dot(s_im, vim_ref[...], preferred_element_type=F32))
    cw = u_ref.shape[-1]
    for i in range(2):
        intra = jnp.dot(u_ref[i], mt_ref[i], preferred_element_type=F32)
        y_ref[i] = (intra + inter[:, i * cw:(i + 1) * cw]).astype(BF16)


def _s5_out(u_g, mt, s_re, s_im, v_re, v_im):
    gp, _, r, cw = u_g.shape
    return pl.pallas_call(
        _s5_out_body,
        grid=(gp,),
        in_specs=[
            pl.BlockSpec((None, 2, r, cw), lambda g: (g, 0, 0, 0)),
            pl.BlockSpec((None, 2, cw, cw), lambda g: (g, 0, 0, 0)),
            pl.BlockSpec((r, LANES), lambda g: (0, g)),
            pl.BlockSpec((r, LANES), lambda g: (0, g)),
            pl.BlockSpec((None, LANES, 2 * cw), lambda g: (g, 0, 0)),
            pl.BlockSpec((None, LANES, 2 * cw), lambda g: (g, 0, 0)),
        ],
        out_specs=pl.BlockSpec((None, 2, r, cw), lambda g: (g, 0, 0, 0)),
        out_shape=jax.ShapeDtypeStruct((gp, 2, r, cw), BF16),
        compiler_params=_cparams(("parallel",)),
        name="s5_toeplitz_out",
    )(u_g, mt, s_re, s_im, v_re, v_im)


def _s5_operators(lambda_re, lambda_im, log_step, b_re, b_im, c_re, c_im, d_skip, chunk):
    hp = lax.Precision.HIGHEST
    g, p, c = b_re.shape
    lam_re = jnp.minimum(lambda_re.astype(F32), SSM_MAX_RE)
    lam_im = lambda_im.astype(F32)
    step = jnp.exp(log_step.astype(F32))[:, None]
    dd = jnp.arange(chunk + 1, dtype=F32)
    mag = jnp.exp((lam_re * step)[..., None] * dd)
    ang = (lam_im * step)[..., None] * dd
    pw_re = mag * jnp.cos(ang)
    pw_im = mag * jnp.sin(ang)
    a_re, a_im = pw_re[..., 1], pw_im[..., 1]
    den = lam_re * lam_re + lam_im * lam_im
    k_re = ((a_re - 1.0) * lam_re + a_im * lam_im) / den
    k_im = (a_im * lam_re - (a_re - 1.0) * lam_im) / den
    bb_re = k_re[..., None] * b_re - k_im[..., None] * b_im
    bb_im = k_re[..., None] * b_im + k_im[..., None] * b_re
    e_re = c_re[:, None] * jnp.moveaxis(pw_re, 2, 1)[:, :, None, :] - c_im[:, None] * jnp.moveaxis(pw_im, 2, 1)[:, :, None, :]
    e_im = c_re[:, None] * jnp.moveaxis(pw_im, 2, 1)[:, :, None, :] + c_im[:, None] * jnp.moveaxis(pw_re, 2, 1)[:, :, None, :]
    kern = (jnp.einsum('gdcp,gpk->gdck', e_re[:, :chunk], bb_re, precision=hp)
            - jnp.einsum('gdcp,gpk->gdck', e_im[:, :chunk], bb_im, precision=hp))
    kern = kern.at[:, 0].add(d_skip.astype(F32)[:, :, None] * jnp.eye(c, dtype=F32))
    idx = jnp.arange(chunk)
    lag = idx[None, :] - idx[:, None]
    kt = jnp.swapaxes(kern, 2, 3)
    mt = jnp.where((lag >= 0)[None, :, :, None, None], kt[:, jnp.clip(lag, 0, chunk - 1)], 0.0)
    mt = jnp.transpose(mt, (0, 1, 3, 2, 4)).reshape(g, chunk * c, chunk * c)
    rv_re = jnp.moveaxis(pw_re[..., :chunk][..., ::-1], 2, 1)
    rv_im = jnp.moveaxis(pw_im[..., :chunk][..., ::-1], 2, 1)
    bt_re = jnp.swapaxes(bb_re, 1, 2)
    bt_im = jnp.swapaxes(bb_im, 1, 2)
    w_re = (rv_re[:, :, None] * bt_re[:, None] - rv_im[:, :, None] * bt_im[:, None]).reshape(g, chunk * c, p)
    w_im = (rv_re[:, :, None] * bt_im[:, None] + rv_im[:, :, None] * bt_re[:, None]).reshape(g, chunk * c, p)
    v_re = jnp.transpose(e_re[:, 1:], (0, 3, 1, 2)).reshape(g, p, chunk * c)
    v_im = -jnp.transpose(e_im[:, 1:], (0, 3, 1, 2)).reshape(g, p, chunk * c)
    gp, cw = g // 2, chunk * c
    z = jnp.zeros((gp, cw, p), F32)
    wr = w_re.reshape(gp, 2, cw, p)
    wi = w_im.reshape(gp, 2, cw, p)
    w_state = jnp.concatenate([
        jnp.concatenate([wr[:, 0], z, wi[:, 0], z], axis=2),
        jnp.concatenate([z, wr[:, 1], z, wi[:, 1]], axis=2)], axis=1)
    zv = jnp.zeros((gp, p, cw), F32)
    vr = v_re.reshape(gp, 2, p, cw)
    vi = v_im.reshape(gp, 2, p, cw)
    v_re_pair = jnp.concatenate([jnp.concatenate([vr[:, 0], zv], axis=2),
                                 jnp.concatenate([zv, vr[:, 1]], axis=2)], axis=1)
    v_im_pair = jnp.concatenate([jnp.concatenate([vi[:, 0], zv], axis=2),
                                 jnp.concatenate([zv, vi[:, 1]], axis=2)], axis=1)
    al_re = pw_re[..., chunk].reshape(1, g * p)
    al_im = pw_im[..., chunk].reshape(1, g * p)
    return dict(mt=mt.reshape(gp, 2, cw, cw).astype(BF16), w_state=w_state.astype(BF16),
                v_re=v_re_pair.astype(BF16), v_im=v_im_pair.astype(BF16), al_re=al_re, al_im=al_im)


def _s5_branch(u2, s0_re, s0_im, ops, *, nb, t, g, c, chunk):
    n_c = t // chunk
    gp, cw = g // 2, chunk * c
    r = n_c * nb
    u_g = u2.reshape(nb, n_c, chunk, gp, 2, c)
    u_g = jnp.transpose(u_g, (3, 4, 1, 0, 2, 5)).reshape(gp, 2, r, cw)
    z_re, z_im = _s5_state(u_g, ops['w_state'])
    p = s0_re.shape[-1]
    w = g * p
    s_re, s_im, f_re, f_im = _s5_scan(z_re.reshape(n_c, nb, w), z_im.reshape(n_c, nb, w),
                                      ops['al_re'], ops['al_im'], s0_re.reshape(nb, w), s0_im.reshape(nb, w))
    y_g = _s5_out(u_g, ops['mt'], s_re.reshape(r, w), s_im.reshape(r, w), ops['v_re'], ops['v_im'])
    y = jnp.transpose(y_g.reshape(gp, 2, n_c, nb, chunk, c), (3, 2, 4, 0, 1, 5)).reshape(nb * t, g * c)
    return y, f_re.reshape(nb, g, p), f_im.reshape(nb, g, p)


def _split_heads_q(q):
    lane = lax.broadcasted_iota(jnp.int32, q.shape, 1)
    zero = jnp.zeros_like(q)
    return jnp.concatenate([jnp.where(lane < 64, q, zero), jnp.where(lane >= 64, q, zero)], axis=0)


def _diff_finish(acc, l, lam, g, out_scale, tq):
    o = acc[:tq] / l[:tq] - lam * (acc[tq:] / l[tq:])
    ms = jnp.mean(o * o, axis=-1, keepdims=True)
    return o * lax.rsqrt(ms + LN_EPS) * g * out_scale


def _attn_prompt_body(lam_ref, q_ref, k_ref, v_ref, g_ref, o_ref, m_sc, acc_sc, *, tq, rc, out_scale):
    qi = pl.program_id(2)
    hw = q_ref.shape[1]
    qz = _split_heads_q(q_ref[...])
    m_sc[...] = jnp.full(m_sc.shape, MASK_VALUE, F32)
    acc_sc[...] = jnp.zeros(acc_sc.shape, F32)
    ones = jnp.ones((tq, hw), BF16)
    dn = (((1,), (1,)), ((), ()))

    def block(kb, masked):
        off = pl.multiple_of(kb * tq, tq)
        k = k_ref[pl.ds(off, tq), :]
        v_ext = jnp.concatenate([v_ref[pl.ds(off, tq), :], ones], axis=1)
        for c in range(2 * tq // rc):
            rows = slice(c * rc, (c + 1) * rc)
            r0 = (c * rc) % tq
            width = r0 + rc if masked else tq
            s = lax.dot_general(qz[rows], k[:width], dn, preferred_element_type=F32)
            if masked:
                row = lax.broadcasted_iota(jnp.int32, s.shape, 0) + r0
                col = lax.broadcasted_iota(jnp.int32, s.shape, 1)
                s = jnp.where((col // ATTN_CHUNK) <= (row // ATTN_CHUNK), s, MASK_VALUE)
            m_prev = m_sc[rows]
            m_new = jnp.maximum(m_prev, jnp.max(s, axis=1, keepdims=True))
            alpha = jnp.exp2(m_prev - m_new)
            p = jnp.exp2(s - pltpu.repeat(m_new, width // hw, axis=1))
            acc_sc[rows] = (pltpu.repeat(alpha, 2, axis=1) * acc_sc[rows]
                            + jnp.dot(p.astype(BF16), v_ext[:width], preferred_element_type=F32))
            m_sc[rows] = m_new

    def full_block(kb, carry):
        block(kb, False)
        return carry

    lax.fori_loop(0, qi, full_block, 0)
    block(qi, True)
    acc = acc_sc[...]
    o_ref[...] = _diff_finish(acc[:, :hw], acc[:, hw:], lam_ref[0], g_ref[...], out_scale, tq).astype(BF16)


def _attn_prompt(lam, q3, k3, v3, g_row, *, heads, tq, out_scale):
    b, t, d = q3.shape
    hw = d // heads
    return pl.pallas_call(
        functools.partial(_attn_prompt_body, tq=tq, rc=min(256, tq), out_scale=out_scale),
        grid=(b, heads, t // tq),
        in_specs=[
            pl.BlockSpec(memory_space=pltpu.SMEM),
            pl.BlockSpec((None, tq, hw), lambda bi, h, qi: (bi, qi, h)),
            pl.BlockSpec((None, t, hw), lambda bi, h, qi: (bi, 0, h)),
            pl.BlockSpec((None, t, hw), lambda bi, h, qi: (bi, 0, h)),
            pl.BlockSpec((1, hw), lambda bi, h, qi: (0, 0)),
        ],
        out_specs=pl.BlockSpec((None, tq, hw), lambda bi, h, qi: (bi, qi, h)),
        out_shape=jax.ShapeDtypeStruct((b, t, d), BF16),
        scratch_shapes=[pltpu.VMEM((2 * tq, hw), F32), pltpu.VMEM((2 * tq, 2 * hw), F32)],
        compiler_params=_cparams(("parallel", "parallel", "arbitrary")),
        name="diff_attn_prompt",
    )(lam, q3, k3, v3, g_row)


def _attn_sample_body(lam_ref, q_ref, kc_ref, kn_ref, vc_ref, vn_ref, g_ref, o_ref, *, heads, out_scale):
    tq = q_ref.shape[0]
    hw = q_ref.shape[1] // heads
    dn = (((1,), (1,)), ((), ()))
    for h in range(heads):
        sl = slice(h * hw, (h + 1) * hw)
        qz = _split_heads_q(q_ref[:, sl])
        s_c = lax.dot_general(qz, kc_ref[:, sl].astype(BF16), dn, preferred_element_type=F32)
        s_n = lax.dot_general(qz, kn_ref[:, sl], dn, preferred_element_type=F32)
        m = jnp.maximum(jnp.max(s_c, axis=1, keepdims=True), jnp.max(s_n, axis=1, keepdims=True))
        p_c = jnp.exp2(s_c - m)
        p_n = jnp.exp2(s_n - m)
        l = jnp.sum(p_c, axis=1, keepdims=True) + jnp.sum(p_n, axis=1, keepdims=True)
        acc = (jnp.dot(p_c.astype(BF16), vc_ref[:, sl].astype(BF16), preferred_element_type=F32)
               + jnp.dot(p_n.astype(BF16), vn_ref[:, sl], preferred_element_type=F32))
        o_ref[:, sl] = _diff_finish(acc, l, lam_ref[0], g_ref[...], out_scale, tq).astype(BF16)


def _attn_sample(lam, q3, kc3, kn3, vc3, vn3, g_row, *, heads, out_scale):
    b, t, d = q3.shape
    past = kc3.shape[1]
    new = pl.BlockSpec((None, t, d), lambda bi: (bi, 0, 0))
    old = pl.BlockSpec((None, past, d), lambda bi: (bi, 0, 0))
    return pl.pallas_call(
        functools.partial(_attn_sample_body, heads=heads, out_scale=out_scale),
        grid=(b,),
        in_specs=[pl.BlockSpec(memory_space=pltpu.SMEM), new, old, new, old, new,
                  pl.BlockSpec((1, d // heads), lambda bi: (0, 0))],
        out_specs=new,
        out_shape=jax.ShapeDtypeStruct((b, t, d), BF16),
        compiler_params=_cparams(("parallel",)),
        name="diff_attn_sample",
    )(lam, q3, kc3, kn3, vc3, vn3, g_row)


def _route(logits, n_groups, epg):
    n_exp = n_groups * epg
    lane = lax.broadcasted_iota(jnp.int32, logits.shape, 1)
    neg = jnp.float32(-jnp.inf)
    is_g = (lane >= n_exp) & (lane < n_exp + n_groups)
    gl = jnp.where(is_g, logits, neg)
    gmax = jnp.max(gl, axis=1, keepdims=True)
    g_idx = jnp.min(jnp.where(gl == gmax, lane - n_exp, n_groups), axis=1, keepdims=True)
    g_w = 1.0 / jnp.sum(jnp.where(is_g, jnp.exp(gl - gmax), 0.0), axis=1, keepdims=True)
    in_grp = (lane < n_exp) & ((lane // epg) == g_idx)
    el = jnp.where(in_grp, logits, neg)
    t1 = jnp.max(el, axis=1, keepdims=True)
    i1 = jnp.min(jnp.where(el == t1, lane, LANES), axis=1, keepdims=True)
    el2 = jnp.where(lane == i1, neg, el)
    t2 = jnp.max(el2, axis=1, keepdims=True)
    i2 = jnp.min(jnp.where(el2 == t2, lane, LANES), axis=1, keepdims=True)
    e2 = jnp.exp(t2 - t1)
    w1 = g_w / (1.0 + e2)
    w2 = g_w * e2 / (1.0 + e2)
    return jnp.where(lane == i1, w1, 0.0) + jnp.where(lane == i2, w2, 0.0)


def _post_body(attn_ref, y_ref, gates_ref, x_ref, wa_ref, wglu_ref, wout_ref, lng_ref, lnb_ref,
               wrh_ref, wrl_ref, h_ref, hb_ref, route_ref, *, d, alpha, n_groups, epg):
    b_attn = jnp.dot(attn_ref[...], wa_ref[...], preferred_element_type=F32)
    yg = jax.nn.gelu(y_ref[...].astype(F32)).astype(BF16)
    glu = jnp.dot(yg, wglu_ref[...], preferred_element_type=F32)
    b_ssm = glu[:, :d] * _sigmoid(glu[:, d:])
    merged = (_sigmoid(gates_ref[:, :d].astype(F32)) * b_ssm
              + _sigmoid(gates_ref[:, d:].astype(F32)) * b_attn)
    mix = jnp.dot(merged.astype(BF16), wout_ref[...], preferred_element_type=F32)
    h = _layer_norm(alpha * x_ref[...] + mix, lng_ref[...], lnb_ref[...])
    h_ref[...] = h
    h_hi = h.astype(BF16)
    hb_ref[...] = h_hi
    h_lo = (h - h_hi.astype(F32)).astype(BF16)
    logits = (jnp.dot(h_hi, wrh_ref[...], preferred_element_type=F32)
              + jnp.dot(h_lo, wrh_ref[...], preferred_element_type=F32)
              + jnp.dot(h_hi, wrl_ref[...], preferred_element_type=F32))
    route_ref[...] = _route(logits, n_groups, epg)


def _post(attn2, y2, gates2, x2, wa, wglu, wout, lng, lnb, wr_hi, wr_lo, *, tm, alpha, n_groups, epg):
    n, d = x2.shape
    row = lambda i: (i, 0)
    wspec = lambda shape: pl.BlockSpec(shape, lambda i: (0, 0), pipeline_mode=pl.Buffered(1))
    return pl.pallas_call(
        functools.partial(_post_body, d=d, alpha=alpha, n_groups=n_groups, epg=epg),
        grid=(n // tm,),
        in_specs=[
            pl.BlockSpec((tm, d), row), pl.BlockSpec((tm, d), row), pl.BlockSpec((tm, 2 * d), row),
            pl.BlockSpec((tm, d), row),
            wspec((d, d)), wspec((d, 2 * d)), wspec((d, d)),
            _const_spec((1, d)), _const_spec((1, d)),
            wspec((d, LANES)), wspec((d, LANES)),
        ],
        out_specs=(pl.BlockSpec((tm, d), row), pl.BlockSpec((tm, d), row), pl.BlockSpec((tm, LANES), row)),
        out_shape=(jax.ShapeDtypeStruct((n, d), F32), jax.ShapeDtypeStruct((n, d), BF16),
                   jax.ShapeDtypeStruct((n, LANES), F32)),
        compiler_params=_cparams(("parallel",)),
        name="merge_outproj_ln_router",
    )(attn2, y2, gates2, x2, wa, wglu, wout, lng, lnb, wr_hi, wr_lo)


def _moe_body(hb_ref, h_ref, route_ref, w1_ref, w3_ref, w2_ref, lng_ref, lnb_ref, o_ref, acc_sc, *, alpha):
    e = pl.program_id(1)

    @pl.when(e == 0)
    def _():
        acc_sc[...] = jnp.zeros(acc_sc.shape, F32)

    x = hb_ref[...]
    a = jnp.dot(x, w1_ref[...], preferred_element_type=F32)
    b = jnp.dot(x, w3_ref[...], preferred_element_type=F32)
    route = route_ref[...]
    lane = lax.broadcasted_iota(jnp.int32, route.shape, 1)
    gate = jnp.sum(jnp.where(lane == e, route, 0.0), axis=1, keepdims=True)
    mid = (a * _sigmoid(a) * b * gate).astype(BF16)
    acc_sc[...] += jnp.dot(mid, w2_ref[...], preferred_element_type=F32)

    @pl.when(e == pl.num_programs(1) - 1)
    def _():
        o_ref[...] = _layer_norm(alpha * h_ref[...] + acc_sc[...], lng_ref[...], lnb_ref[...])


def _moe(hb, h, route, w1, w3, w2, lng, lnb, *, tm, alpha):
    n, d = h.shape
    n_exp, _, de = w1.shape
    row = lambda i, e: (i, 0)
    return pl.pallas_call(
        functools.partial(_moe_body, alpha=alpha),
        grid=(n // tm, n_exp),
        in_specs=[
            pl.BlockSpec((tm, d), row), pl.BlockSpec((tm, d), row), pl.BlockSpec((tm, LANES), row),
            pl.BlockSpec((None, d, de), lambda i, e: (e, 0, 0)),
            pl.BlockSpec((None, d, de), lambda i, e: (e, 0, 0)),
            pl.BlockSpec((None, de, d), lambda i, e: (e, 0, 0)),
            pl.BlockSpec((1, d), lambda i, e: (0, 0)), pl.BlockSpec((1, d), lambda i, e: (0, 0)),
        ],
        out_specs=pl.BlockSpec((tm, d), row),
        out_shape=jax.ShapeDtypeStruct((n, d), F32),
        scratch_shapes=[pltpu.VMEM((tm, d), F32)],
        compiler_params=_cparams(("parallel", "arbitrary")),
        name="moe_experts_ln",
    )(hb, h, route, w1, w3, w2, lng, lnb)


def _rope_tables(pos, rows):
    half = 32
    inv_freq = ROPE_THETA ** (-jnp.arange(half, dtype=F32) / half)
    ang = pos.astype(F32)[:, None] * inv_freq[None, :]
    cos = jnp.cos(ang)
    sin = jnp.sin(ang)
    cos_t = jnp.tile(jnp.concatenate([cos, cos], axis=1), (1, LANES // 64))
    sin_t = jnp.tile(jnp.concatenate([-sin, sin], axis=1), (1, LANES // 64))
    reps = max(1, rows // cos_t.shape[0])
    return jnp.tile(cos_t, (reps, 1)), jnp.tile(sin_t, (reps, 1))


def _encoder_layer(x, pos, k_past, v_past, s_re, s_im, p, lam_init, alpha):
    nb, t, d = x.shape
    n = nb * t
    heads = p['heads']
    g, pstate = p['groups'], p['pstate']
    tm = min(512, n)
    x2 = x.reshape(n, d)
    cos_t, sin_t = _rope_tables(pos, tm)
    head_dim = d // heads // 2
    u, q, kf, kb, vf, vb, gates = _inproj(x2, p['w_in'], cos_t, sin_t, tm=tm, q_scale=head_dim ** -0.5 * math.log2(math.e))

    out_scale = 1.0 - lam_init
    if k_past is None:
        attn = _attn_prompt(p['lam'], q.reshape(nb, t, d), kb.reshape(nb, t, d), vb.reshape(nb, t, d),
                            p['subln_g'], heads=heads, tq=min(512, t), out_scale=out_scale)
    else:
        past = k_past.shape[1]
        attn = _attn_sample(p['lam'], q.reshape(nb, t, d), k_past.reshape(nb, past, d), kb.reshape(nb, t, d),
                            v_past.reshape(nb, past, d), vb.reshape(nb, t, d), p['subln_g'],
                            heads=heads, out_scale=out_scale)

    y, f_re, f_im = _s5_branch(u, s_re, s_im, p['s5'], nb=nb, t=t, g=g, c=d // g, chunk=S5_CHUNK)

    h, hb, route = _post(attn.reshape(n, d), y, gates, x2, p['w_attn_branch'], p['w_glu'], p['w_out'],
                         p['ln1_g'], p['ln1_b'], p['wr_hi'], p['wr_lo'],
                         tm=tm, alpha=alpha, n_groups=p['n_groups'], epg=p['epg'])
    out = _moe(hb, h, route, p['w1'], p['w3'], p['w2'], p['ln2_g'], p['ln2_b'], tm=min(1024, n), alpha=alpha)
    k_new = kf.reshape(nb, t, heads, 2, head_dim)
    v_new = vf.reshape(nb, t, heads, 2 * head_dim)
    return out.reshape(nb, t, d), k_new, v_new, f_re, f_im


def kernel(x_prompt, x_sample, cache_k, cache_v, state_ssm_re, state_ssm_im, w_in, lambda_re, lambda_im, log_step, b_re, b_im, c_re, c_im, d_skip, w_glu, lambda_q1, lambda_k1, lambda_q2, lambda_k2, subln_g, w_attn_branch, w_out, ln1_g, ln1_b, w_router_group, w_router_expert, w1, w3, w2, ln2_g, ln2_b):
    depth = w_in.shape[0]
    d = x_prompt.shape[-1]
    heads = cache_k.shape[3]
    past_len = cache_k.shape[2]
    groups, pstate = lambda_re.shape[1], lambda_re.shape[2]
    n_groups, epg = w_router_expert.shape[2], w_router_expert.shape[3]
    n_exp = n_groups * epg
    alpha = (2.0 * depth) ** 0.25
    pos_prompt = jnp.arange(x_prompt.shape[1], dtype=jnp.int32)
    pos_sample = past_len + jnp.arange(x_sample.shape[1], dtype=jnp.int32)

    h_p, h_s = x_prompt, x_sample
    outs = [[] for _ in range(8)]
    for l in range(depth):
        lam_init = 0.8 - 0.6 * math.exp(-0.3 * l)
        lam = (jnp.exp(jnp.sum(lambda_q1[l].astype(F32) * lambda_k1[l].astype(F32)))
               - jnp.exp(jnp.sum(lambda_q2[l].astype(F32) * lambda_k2[l].astype(F32))) + lam_init)
        wr = jnp.concatenate([w_router_expert[l].reshape(d, n_exp).astype(F32), w_router_group[l].astype(F32),
                              jnp.zeros((d, LANES - n_exp - n_groups), F32)], axis=1)
        wr_hi = wr.astype(BF16)
        p = dict(
            heads=heads, groups=groups, pstate=pstate, n_groups=n_groups, epg=epg,
            w_in=w_in[l].astype(BF16), lam=lam.reshape(1).astype(F32),
            subln_g=subln_g[l].astype(F32).reshape(1, -1),
            s5=_s5_operators(lambda_re[l], lambda_im[l], log_step[l], b_re[l].astype(F32), b_im[l].astype(F32),
                             c_re[l].astype(F32), c_im[l].astype(F32), d_skip[l], S5_CHUNK),
            w_attn_branch=w_attn_branch[l].astype(BF16), w_glu=w_glu[l].astype(BF16), w_out=w_out[l].astype(BF16),
            ln1_g=ln1_g[l].astype(F32).reshape(1, d), ln1_b=ln1_b[l].astype(F32).reshape(1, d),
            ln2_g=ln2_g[l].astype(F32).reshape(1, d), ln2_b=ln2_b[l].astype(F32).reshape(1, d),
            wr_hi=wr_hi, wr_lo=(wr - wr_hi.astype(F32)).astype(BF16),
            w1=w1[l].astype(BF16), w3=w3[l].astype(BF16), w2=w2[l].astype(BF16),
        )
        zero_state = jnp.zeros((h_p.shape[0], groups, pstate), F32)
        h_p, k_p, v_p, sr_p, si_p = _encoder_layer(h_p, pos_prompt, None, None, zero_state, zero_state, p, lam_init, alpha)
        h_s, k_s, v_s, sr_s, si_s = _encoder_layer(h_s, pos_sample, cache_k[l], cache_v[l],
                                                   state_ssm_re[l].astype(F32), state_ssm_im[l].astype(F32),
                                                   p, lam_init, alpha)
        for lst, val in zip(outs, (k_p.astype(cache_k.dtype), v_p.astype(cache_v.dtype),
                                   sr_p.astype(state_ssm_re.dtype), si_p.astype(state_ssm_im.dtype),
                                   k_s.astype(cache_k.dtype), v_s.astype(cache_v.dtype),
                                   sr_s.astype(state_ssm_re.dtype), si_s.astype(state_ssm_im.dtype))):
            lst.append(val)
    return (h_p, h_s) + tuple(jnp.stack(o) for o in outs)
```

```python
import functools
import math

import jax
import jax.numpy as jnp
from jax import lax
from jax.experimental import pallas as pl
from jax.experimental.pallas import tpu as pltpu

F32 = jnp.float32
BF16 = jnp.bfloat16

LANES = 128
VMEM_LIMIT = 56 * 1024 * 1024

ATTN_CHUNK = 64
ROPE_THETA = 10000.0
MASK_VALUE = -1e30
SSM_MAX_RE = -1e-4
LN_EPS = 1e-5
S5_CHUNK = 32


def _cparams(sem):
    return pltpu.CompilerParams(dimension_semantics=sem, vmem_limit_bytes=VMEM_LIMIT)


def _const_spec(shape):
    nd = len(shape)
    return pl.BlockSpec(shape, lambda *_: (0,) * nd)


def _layer_norm(z, g, b):
    mu = jnp.mean(z, axis=-1, keepdims=True)
    zc = z - mu
    var = jnp.mean(zc * zc, axis=-1, keepdims=True)
    return zc * lax.rsqrt(var + LN_EPS) * g + b


def _sigmoid(x):
    return 1.0 / (1.0 + jnp.exp(-x))


def _inproj_body(x_ref, w_ref, cos_ref, sin_ref, u_ref, q_ref, kf_ref, kb_ref, vf_ref, vb_ref, g_ref,
                 *, d, q_scale):
    x = x_ref[...].astype(BF16)
    cos = cos_ref[...]
    sin = sin_ref[...]
    lane = lax.broadcasted_iota(jnp.int32, cos.shape, 1)
    first_half = (lane % 64) < 32

    def proj(j):
        return jnp.dot(x, w_ref[:, j * d:(j + 1) * d], preferred_element_type=F32)

    def rope(blk):
        outs = []
        for c in range(d // LANES):
            sub = blk[:, c * LANES:(c + 1) * LANES]
            partner = jnp.where(first_half, pltpu.roll(sub, 96, 1), pltpu.roll(sub, 32, 1))
            outs.append(sub * cos + partner * sin)
        return jnp.concatenate(outs, axis=1)

    u_ref[...] = proj(0).astype(BF16)
    q_ref[...] = (rope(proj(1)) * q_scale).astype(BF16)
    k = rope(proj(2))
    kf_ref[...] = k
    kb_ref[...] = k.astype(BF16)
    v = proj(3)
    vf_ref[...] = v
    vb_ref[...] = v.astype(BF16)
    g_ref[:, :d] = proj(4).astype(BF16)
    g_ref[:, d:] = proj(5).astype(BF16)


def _inproj(x2, w_bf, cos_t, sin_t, *, tm, q_scale):
    n, d = x2.shape
    width = w_bf.shape[1]
    n_tab = cos_t.shape[0] // tm
    row = lambda i: (i, 0)
    tab = lambda i: (i % n_tab, 0)
    outs = (
        jax.ShapeDtypeStruct((n, d), BF16),
        jax.ShapeDtypeStruct((n, d), BF16),
        jax.ShapeDtypeStruct((n, d), F32),
        jax.ShapeDtypeStruct((n, d), BF16),
        jax.ShapeDtypeStruct((n, d), F32),
        jax.ShapeDtypeStruct((n, d), BF16),
        jax.ShapeDtypeStruct((n, 2 * d), BF16),
    )
    return pl.pallas_call(
        functools.partial(_inproj_body, d=d, q_scale=q_scale),
        grid=(n // tm,),
        in_specs=[
            pl.BlockSpec((tm, d), row),
            pl.BlockSpec((d, width), lambda i: (0, 0), pipeline_mode=pl.Buffered(1)),
            pl.BlockSpec((tm, LANES), tab),
            pl.BlockSpec((tm, LANES), tab),
        ],
        out_specs=(
            pl.BlockSpec((tm, d), row), pl.BlockSpec((tm, d), row), pl.BlockSpec((tm, d), row),
            pl.BlockSpec((tm, d), row), pl.BlockSpec((tm, d), row), pl.BlockSpec((tm, d), row),
            pl.BlockSpec((tm, 2 * d), row),
        ),
        out_shape=outs,
        compiler_params=_cparams(("parallel",)),
        name="inproj_rope",
    )(x2, w_bf, cos_t, sin_t)


def _s5_state_body(u_ref, w_ref, zre_ref, zim_ref):
    u = jnp.concatenate([u_ref[0], u_ref[1]], axis=1)
    z = jnp.dot(u, w_ref[...], preferred_element_type=F32)
    zre_ref[...] = z[:, :LANES]
    zim_ref[...] = z[:, LANES:]


def _s5_state(u_g, w_state):
    gp, _, r, cw = u_g.shape
    return pl.pallas_call(
        _s5_state_body,
        grid=(gp,),
        in_specs=[
            pl.BlockSpec((None, 2, r, cw), lambda g: (g, 0, 0, 0)),
            pl.BlockSpec((None, 2 * cw, 2 * LANES), lambda g: (g, 0, 0)),
        ],
        out_specs=(pl.BlockSpec((r, LANES), lambda g: (0, g)), pl.BlockSpec((r, LANES), lambda g: (0, g))),
        out_shape=(jax.ShapeDtypeStruct((r, gp * LANES), F32), jax.ShapeDtypeStruct((r, gp * LANES), F32)),
        compiler_params=_cparams(("parallel",)),
        name="s5_chunk_state",
    )(u_g, w_state)


def _s5_scan_body(zre_ref, zim_ref, are_ref, aim_ref, s0re_ref, s0im_ref,
                  sre_ref, sim_ref, fre_ref, fim_ref, *, n_c):
    ar = are_ref[...]
    ai = aim_ref[...]

    def step(c, carry):
        s_re, s_im = carry
        sre_ref[c] = s_re
        sim_ref[c] = s_im
        n_re = ar * s_re - ai * s_im + zre_ref[c]
        n_im = ar * s_im + ai * s_re + zim_ref[c]
        return n_re, n_im

    f_re, f_im = lax.fori_loop(0, n_c, step, (s0re_ref[...], s0im_ref[...]))
    fre_ref[...] = f_re
    fim_ref[...] = f_im


def _s5_scan(z_re, z_im, a_re, a_im, s0_re, s0_im, *, lane_blk=256):
    n_c, nb, w = z_re.shape
    seq = pl.BlockSpec((n_c, nb, lane_blk), lambda j: (0, 0, j))
    vec = pl.BlockSpec((1, lane_blk), lambda j: (0, j))
    st = pl.BlockSpec((nb, lane_blk), lambda j: (0, j))
    return pl.pallas_call(
        functools.partial(_s5_scan_body, n_c=n_c),
        grid=(w // lane_blk,),
        in_specs=[seq, seq, vec, vec, st, st],
        out_specs=(seq, seq, st, st),
        out_shape=(jax.ShapeDtypeStruct((n_c, nb, w), F32), jax.ShapeDtypeStruct((n_c, nb, w), F32),
                   jax.ShapeDtypeStruct((nb, w), F32), jax.ShapeDtypeStruct((nb, w), F32)),
        compiler_params=_cparams(("parallel",)),
        name="s5_chunk_scan",
    )(z_re, z_im, a_re, a_im, s0_re, s0_im)


def _s5_out_body(u_ref, mt_ref, sre_ref, sim_ref, vre_ref, vim_ref, y_ref):
    s_re = sre_ref[...].astype(BF16)
    s_im = sim_ref[...].astype(BF16)
    inter = (jnp.dot(s_re, vre_ref[...], preferred_element_type=F32)
             + jnp.dot(s_im, vim_ref[...], preferred_element_type=F32))
    cw = u_ref.shape[-1]
    for i in range(2):
        intra = jnp.dot(u_ref[i], mt_ref[i], preferred_element_type=F32)
        y_ref[i] = (intra + inter[:, i * cw:(i + 1) * cw]).astype(BF16)


def _s5_out(u_g, mt, s_re, s_im, v_re, v_im):
    gp, _, r, cw = u_g.shape
    return pl.pallas_call(
        _s5_out_body,
        grid=(gp,),
        in_specs=[
            pl.BlockSpec((None, 2, r, cw), lambda g: (g, 0, 0, 0)),
            pl.BlockSpec((None, 2, cw, cw), lambda g: (g, 0, 0, 0)),
            pl.BlockSpec((r, LANES), lambda g: (0, g)),
            pl.BlockSpec((r, LANES), lambda g: (0, g)),
            pl.BlockSpec((None, LANES, 2 * cw), lambda g: (g, 0, 0)),
            pl.BlockSpec((None, LANES, 2 * cw), lambda g: (g, 0, 0)),
        ],
        out_specs=pl.BlockSpec((None, 2, r, cw), lambda g: (g, 0, 0, 0)),
        out_shape=jax.ShapeDtypeStruct((gp, 2, r, cw), BF16),
        compiler_params=_cparams(("parallel",)),
        name="s5_toeplitz_out",
    )(u_g, mt, s_re, s_im, v_re, v_im)


def _s5_operators(lambda_re, lambda_im, log_step, b_re, b_im, c_re, c_im, d_skip, chunk):
    hp = lax.Precision.HIGHEST
    g, p, c = b_re.shape
    lam_re = jnp.minimum(lambda_re.astype(F32), SSM_MAX_RE)
    lam_im = lambda_im.astype(F32)
    step = jnp.exp(log_step.astype(F32))[:, None]
    dd = jnp.arange(chunk + 1, dtype=F32)
    mag = jnp.exp((lam_re * step)[..., None] * dd)
    ang = (lam_im * step)[..., None] * dd
    pw_re = mag * jnp.cos(ang)
    pw_im = mag * jnp.sin(ang)
    a_re, a_im = pw_re[..., 1], pw_im[..., 1]
    den = lam_re * lam_re + lam_im * lam_im
    k_re = ((a_re - 1.0) * lam_re + a_im * lam_im) / den
    k_im = (a_im * lam_re - (a_re - 1.0) * lam_im) / den
    bb_re = k_re[..., None] * b_re - k_im[..., None] * b_im
    bb_im = k_re[..., None] * b_im + k_im[..., None] * b_re
    e_re = c_re[:, None] * jnp.moveaxis(pw_re, 2, 1)[:, :, None, :] - c_im[:, None] * jnp.moveaxis(pw_im, 2, 1)[:, :, None, :]
    e_im = c_re[:, None] * jnp.moveaxis(pw_im, 2, 1)[:, :, None, :] + c_im[:, None] * jnp.moveaxis(pw_re, 2, 1)[:, :, None, :]
    kern = (jnp.einsum('gdcp,gpk->gdck', e_re[:, :chunk], bb_re, precision=hp)
            - jnp.einsum('gdcp,gpk->gdck', e_im[:, :chunk], bb_im, precision=hp))
    kern = kern.at[:, 0].add(d_skip.astype(F32)[:, :, None] * jnp.eye(c, dtype=F32))
    idx = jnp.arange(chunk)
    lag = idx[None, :] - idx[:, None]
    kt = jnp.swapaxes(kern, 2, 3)
    mt = jnp.where((lag >= 0)[None, :, :, None, None], kt[:, jnp.clip(lag, 0, chunk - 1)], 0.0)
    mt = jnp.transpose(mt, (0, 1, 3, 2, 4)).reshape(g, chunk * c, chunk * c)
    rv_re = jnp.moveaxis(pw_re[..., :chunk][..., ::-1], 2, 1)
    rv_im = jnp.moveaxis(pw_im[..., :chunk][..., ::-1], 2, 1)
    bt_re = jnp.swapaxes(bb_re, 1, 2)
    bt_im = jnp.swapaxes(bb_im, 1, 2)
    w_re = (rv_re[:, :, None] * bt_re[:, None] - rv_im[:, :, None] * bt_im[:, None]).reshape(g, chunk * c, p)
    w_im = (rv_re[:, :, None] * bt_im[:, None] + rv_im[:, :, None] * bt_re[:, None]).reshape(g, chunk * c, p)
    v_re = jnp.transpose(e_re[:, 1:], (0, 3, 1, 2)).reshape(g, p, chunk * c)
    v_im = -jnp.transpose(e_im[:, 1:], (0, 3, 1, 2)).reshape(g, p, chunk * c)
    gp, cw = g // 2, chunk * c
    z = jnp.zeros((gp, cw, p), F32)
    wr = w_re.reshape(gp, 2, cw, p)
    wi = w_im.reshape(gp, 2, cw, p)
    w_state = jnp.concatenate([
        jnp.concatenate([wr[:, 0], z, wi[:, 0], z], axis=2),
        jnp.concatenate([z, wr[:, 1], z, wi[:, 1]], axis=2)], axis=1)
    zv = jnp.zeros((gp, p, cw), F32)
    vr = v_re.reshape(gp, 2, p, cw)
    vi = v_im.reshape(gp, 2, p, cw)
    v_re_pair = jnp.concatenate([jnp.concatenate([vr[:, 0], zv], axis=2),
                                 jnp.concatenate([zv, vr[:, 1]], axis=2)], axis=1)
    v_im_pair = jnp.concatenate([jnp.concatenate([vi[:, 0], zv], axis=2),
                                 jnp.concatenate([zv, vi[:, 1]], axis=2)], axis=1)
    al_re = pw_re[..., chunk].reshape(1, g * p)
    al_im = pw_im[..., chunk].reshape(1, g * p)
    return dict(mt=mt.reshape(gp, 2, cw, cw).astype(BF16), w_state=w_state.astype(BF16),
                v_re=v_re_pair.astype(BF16), v_im=v_im_pair.astype(BF16), al_re=al_re, al_im=al_im)


def _s5_branch(u2, s0_re, s0_im, ops, *, nb, t, g, c, chunk):
    n_c = t // chunk
    gp, cw = g // 2, chunk * c
    r = n_c * nb
    u_g = u2.reshape(nb, n_c, chunk, gp, 2, c)
    u_g = jnp.transpose(u_g, (3, 4, 1, 0, 2, 5)).reshape(gp, 2, r, cw)
    z_re, z_im = _s5_state(u_g, ops['w_state'])
    p = s0_re.shape[-1]
    w = g * p
    s_re, s_im, f_re, f_im = _s5_scan(z_re.reshape(n_c, nb, w), z_im.reshape(n_c, nb, w),
                                      ops['al_re'], ops['al_im'], s0_re.reshape(nb, w), s0_im.reshape(nb, w))
    y_g = _s5_out(u_g, ops['mt'], s_re.reshape(r, w), s_im.reshape(r, w), ops['v_re'], ops['v_im'])
    y = jnp.transpose(y_g.reshape(gp, 2, n_c, nb, chunk, c), (3, 2, 4, 0, 1, 5)).reshape(nb * t, g * c)
    return y, f_re.reshape(nb, g, p), f_im.reshape(nb, g, p)


def _split_heads_q(q):
    lane = lax.broadcasted_iota(jnp.int32, q.shape, 1)
    zero = jnp.zeros_like(q)
    return jnp.concatenate([jnp.where(lane < 64, q, zero), jnp.where(lane >= 64, q, zero)], axis=0)


def _diff_finish(acc, l, lam, g, out_scale, tq):
    o = acc[:tq] / l[:tq] - lam * (acc[tq:] / l[tq:])
    ms = jnp.mean(o * o, axis=-1, keepdims=True)
    return o * lax.rsqrt(ms + LN_EPS) * g * out_scale


def _attn_prompt_body(lam_ref, q_ref, k_ref, v_ref, g_ref, o_ref, m_sc, acc_sc, *, tq, rc, out_scale):
    qi = pl.program_id(2)
    hw = q_ref.shape[1]
    qz = _split_heads_q(q_ref[...])
    m_sc[...] = jnp.full(m_sc.shape, MASK_VALUE, F32)
    acc_sc[...] = jnp.zeros(acc_sc.shape, F32)
    ones = jnp.ones((tq, hw), BF16)
    dn = (((1,), (1,)), ((), ()))

    def block(kb, masked):
        off = pl.multiple_of(kb * tq, tq)
        k = k_ref[pl.ds(off, tq), :]
        v_ext = jnp.concatenate([v_ref[pl.ds(off, tq), :], ones], axis=1)
        rcb = min(2 * rc, tq) if masked else rc
        for c in range(2 * tq // rcb):
            rows = slice(c * rcb, (c + 1) * rcb)
            r0 = (c * rcb) % tq
            width = r0 + rcb if masked else tq
            s = lax.dot_general(qz[rows], k[:width], dn, preferred_element_type=F32)
            if masked:
                row = lax.broadcasted_iota(jnp.int32, s.shape, 0) + r0
                col = lax.broadcasted_iota(jnp.int32, s.shape, 1)
                s = jnp.where((col // ATTN_CHUNK) <= (row // ATTN_CHUNK), s, MASK_VALUE)
            m_prev = m_sc[rows]
            m_new = jnp.maximum(m_prev, jnp.max(s, axis=1, keepdims=True))
            alpha = jnp.exp2(m_prev - m_new)
            p = jnp.exp2(s - jnp.concatenate([m_new] * (width // hw), axis=1))
            acc_sc[rows] = (jnp.concatenate([alpha, alpha], axis=1) * acc_sc[rows]
                            + jnp.dot(p.astype(BF16), v_ext[:width], preferred_element_type=F32))
            m_sc[rows] = m_new

    def full_block(kb, carry):
        block(kb, False)
        return carry

    lax.fori_loop(0, qi, full_block, 0)
    block(qi, True)
    acc = acc_sc[...]
    o_ref[...] = _diff_finish(acc[:, :hw], acc[:, hw:], lam_ref[0], g_ref[...], out_scale, tq).astype(BF16)


def _attn_prompt(lam, q3, k3, v3, g_row, *, heads, tq, out_scale):
    b, t, d = q3.shape
    hw = d // heads
    return pl.pallas_call(
        functools.partial(_attn_prompt_body, tq=tq, rc=min(128, tq), out_scale=out_scale),
        grid=(b, heads, t // tq),
        in_specs=[
            pl.BlockSpec(memory_space=pltpu.SMEM),
            pl.BlockSpec((None, tq, hw), lambda bi, h, qi: (bi, qi, h)),
            pl.BlockSpec((None, t, hw), lambda bi, h, qi: (bi, 0, h)),
            pl.BlockSpec((None, t, hw), lambda bi, h, qi: (bi, 0, h)),
            pl.BlockSpec((1, hw), lambda bi, h, qi: (0, 0)),
        ],
        out_specs=pl.BlockSpec((None, tq, hw), lambda bi, h, qi: (bi, qi, h)),
        out_shape=jax.ShapeDtypeStruct((b, t, d), BF16),
        scratch_shapes=[pltpu.VMEM((2 * tq, hw), F32), pltpu.VMEM((2 * tq, 2 * hw), F32)],
        compiler_params=_cparams(("parallel", "parallel", "arbitrary")),
        name="diff_attn_prompt",
    )(lam, q3, k3, v3, g_row)


def _attn_sample_body(lam_ref, q_ref, kc_ref, kn_ref, vc_ref, vn_ref, g_ref, o_ref, *, heads, out_scale):
    tq = q_ref.shape[0]
    hw = q_ref.shape[1] // heads
    dn = (((1,), (1,)), ((), ()))
    for h in range(heads):
        sl = slice(h * hw, (h + 1) * hw)
        qz = _split_heads_q(q_ref[:, sl])
        s_c = lax.dot_general(qz, kc_ref[:, sl].astype(BF16), dn, preferred_element_type=F32)
        s_n = lax.dot_general(qz, kn_ref[:, sl], dn, preferred_element_type=F32)
        m = jnp.maximum(jnp.max(s_c, axis=1, keepdims=True), jnp.max(s_n, axis=1, keepdims=True))
        p_c = jnp.exp2(s_c - m)
        p_n = jnp.exp2(s_n - m)
        l = jnp.sum(p_c, axis=1, keepdims=True) + jnp.sum(p_n, axis=1, keepdims=True)
        acc = (jnp.dot(p_c.astype(BF16), vc_ref[:, sl].astype(BF16), preferred_element_type=F32)
               + jnp.dot(p_n.astype(BF16), vn_ref[:, sl], preferred_element_type=F32))
        o_ref[:, sl] = _diff_finish(acc, l, lam_ref[0], g_ref[...], out_scale, tq).astype(BF16)


def _attn_sample(lam, q3, kc3, kn3, vc3, vn3, g_row, *, heads, out_scale):
    b, t, d = q3.shape
    past = kc3.shape[1]
    new = pl.BlockSpec((None, t, d), lambda bi: (bi, 0, 0))
    old = pl.BlockSpec((None, past, d), lambda bi: (bi, 0, 0))
    return pl.pallas_call(
        functools.partial(_attn_sample_body, heads=heads, out_scale=out_scale),
        grid=(b,),
        in_specs=[pl.BlockSpec(memory_space=pltpu.SMEM), new, old, new, old, new,
                  pl.BlockSpec((1, d // heads), lambda bi: (0, 0))],
        out_specs=new,
        out_shape=jax.ShapeDtypeStruct((b, t, d), BF16),
        compiler_params=_cparams(("parallel",)),
        name="diff_attn_sample",
    )(lam, q3, kc3, kn3, vc3, vn3, g_row)


def _route(logits, n_groups, epg):
    n_exp = n_groups * epg
    lane = lax.broadcasted_iota(jnp.int32, logits.shape, 1)
    neg = jnp.float32(-jnp.inf)
    is_g = (lane >= n_exp) & (lane < n_exp + n_groups)
    gl = jnp.where(is_g, logits, neg)
    gmax = jnp.max(gl, axis=1, keepdims=True)
    g_idx = jnp.min(jnp.where(gl == gmax, lane - n_exp, n_groups), axis=1, keepdims=True)
    g_w = 1.0 / jnp.sum(jnp.where(is_g, jnp.exp(gl - gmax), 0.0), axis=1, keepdims=True)
    in_grp = (lane < n_exp) & ((lane // epg) == g_idx)
    el = jnp.where(in_grp, logits, neg)
    t1 = jnp.max(el, axis=1, keepdims=True)
    i1 = jnp.min(jnp.where(el == t1, lane, LANES), axis=1, keepdims=True)
    el2 = jnp.where(lane == i1, neg, el)
    t2 = jnp.max(el2, axis=1, keepdims=True)
    i2 = jnp.min(jnp.where(el2 == t2, lane, LANES), axis=1, keepdims=True)
    e2 = jnp.exp(t2 - t1)
    w1 = g_w / (1.0 + e2)
    w2 = g_w * e2 / (1.0 + e2)
    return jnp.where(lane == i1, w1, 0.0) + jnp.where(lane == i2, w2, 0.0)


def _post_body(attn_ref, y_ref, gates_ref, x_ref, wa_ref, wglu_ref, wout_ref, lng_ref, lnb_ref,
               wrh_ref, wrl_ref, h_ref, hb_ref, route_ref, *, d, alpha, n_groups, epg):
    b_attn = jnp.dot(attn_ref[...], wa_ref[...], preferred_element_type=F32)
    yg = jax.nn.gelu(y_ref[...].astype(F32)).astype(BF16)
    glu = jnp.dot(yg, wglu_ref[...], preferred_element_type=F32)
    b_ssm = glu[:, :d] * _sigmoid(glu[:, d:])
    merged = (_sigmoid(gates_ref[:, :d].astype(F32)) * b_ssm
              + _sigmoid(gates_ref[:, d:].astype(F32)) * b_attn)
    mix = jnp.dot(merged.astype(BF16), wout_ref[...], preferred_element_type=F32)
    h = _layer_norm(alpha * x_ref[...] + mix, lng_ref[...], lnb_ref[...])
    h_ref[...] = h
    h_hi = h.astype(BF16)
    hb_ref[...] = h_hi
    h_lo = (h - h_hi.astype(F32)).astype(BF16)
    logits = (jnp.dot(h_hi, wrh_ref[...], preferred_element_type=F32)
              + jnp.dot(h_lo, wrh_ref[...], preferred_element_type=F32)
              + jnp.dot(h_hi, wrl_ref[...], preferred_element_type=F32))
    route_ref[...] = _route(logits, n_groups, epg)


def _post(attn2, y2, gates2, x2, wa, wglu, wout, lng, lnb, wr_hi, wr_lo, *, tm, alpha, n_groups, epg):
    n, d = x2.shape
    row = lambda i: (i, 0)
    wspec = lambda shape: pl.BlockSpec(shape, lambda i: (0, 0), pipeline_mode=pl.Buffered(1))
    return pl.pallas_call(
        functools.partial(_post_body, d=d, alpha=alpha, n_groups=n_groups, epg=epg),
        grid=(n // tm,),
        in_specs=[
            pl.BlockSpec((tm, d), row), pl.BlockSpec((tm, d), row), pl.BlockSpec((tm, 2 * d), row),
            pl.BlockSpec((tm, d), row),
            wspec((d, d)), wspec((d, 2 * d)), wspec((d, d)),
            _const_spec((1, d)), _const_spec((1, d)),
            wspec((d, LANES)), wspec((d, LANES)),
        ],
        out_specs=(pl.BlockSpec((tm, d), row), pl.BlockSpec((tm, d), row), pl.BlockSpec((tm, LANES), row)),
        out_shape=(jax.ShapeDtypeStruct((n, d), F32), jax.ShapeDtypeStruct((n, d), BF16),
                   jax.ShapeDtypeStruct((n, LANES), F32)),
        compiler_params=_cparams(("parallel",)),
        name="merge_outproj_ln_router",
    )(attn2, y2, gates2, x2, wa, wglu, wout, lng, lnb, wr_hi, wr_lo)


def _moe_body(hb_ref, h_ref, route_ref, w1_ref, w3_ref, w2_ref, lng_ref, lnb_ref, o_ref, acc_sc, *, alpha):
    e = pl.program_id(1)

    @pl.when(e == 0)
    def _():
        acc_sc[...] = jnp.zeros(acc_sc.shape, F32)

    x = hb_ref[...]
    a = jnp.dot(x, w1_ref[...], preferred_element_type=F32)
    b = jnp.dot(x, w3_ref[...], preferred_element_type=F32)
    route = route_ref[...]
    lane = lax.broadcasted_iota(jnp.int32, route.shape, 1)
    gate = jnp.sum(jnp.where(lane == e, route, 0.0), axis=1, keepdims=True)
    mid = (a * _sigmoid(a) * b * gate).astype(BF16)
    acc_sc[...] += jnp.dot(mid, w2_ref[...], preferred_element_type=F32)

    @pl.when(e == pl.num_programs(1) - 1)
    def _():
        o_ref[...] = _layer_norm(alpha * h_ref[...] + acc_sc[...], lng_ref[...], lnb_ref[...])


def _moe(hb, h, route, w1, w3, w2, lng, lnb, *, tm, alpha):
    n, d = h.shape
    n_exp, _, de = w1.shape
    row = lambda i, e: (i, 0)
    return pl.pallas_call(
        functools.partial(_moe_body, alpha=alpha),
        grid=(n // tm, n_exp),
        in_specs=[
            pl.BlockSpec((tm, d), row), pl.BlockSpec((tm, d), row), pl.BlockSpec((tm, LANES), row),
            pl.BlockSpec((None, d, de), lambda i, e: (e, 0, 0)),
            pl.BlockSpec((None, d, de), lambda i, e: (e, 0, 0)),
            pl.BlockSpec((None, de, d), lambda i, e: (e, 0, 0)),
            pl.BlockSpec((1, d), lambda i, e: (0, 0)), pl.BlockSpec((1, d), lambda i, e: (0, 0)),
        ],
        out_specs=pl.BlockSpec((tm, d), row),
        out_shape=jax.ShapeDtypeStruct((n, d), F32),
        scratch_shapes=[pltpu.VMEM((tm, d), F32)],
        compiler_params=_cparams(("parallel", "arbitrary")),
        name="moe_experts_ln",
    )(hb, h, route, w1, w3, w2, lng, lnb)


def _rope_tables(pos, rows):
    half = 32
    inv_freq = ROPE_THETA ** (-jnp.arange(half, dtype=F32) / half)
    ang = pos.astype(F32)[:, None] * inv_freq[None, :]
    cos = jnp.cos(ang)
    sin = jnp.sin(ang)
    cos_t = jnp.tile(jnp.concatenate([cos, cos], axis=1), (1, LANES // 64))
    sin_t = jnp.tile(jnp.concatenate([-sin, sin], axis=1), (1, LANES // 64))
    reps = max(1, rows // cos_t.shape[0])
    return jnp.tile(cos_t, (reps, 1)), jnp.tile(sin_t, (reps, 1))


def _encoder_layer(x, pos, k_past, v_past, s_re, s_im, p, lam_init, alpha):
    nb, t, d = x.shape
    n = nb * t
    heads = p['heads']
    g, pstate = p['groups'], p['pstate']
    tm = min(512, n)
    x2 = x.reshape(n, d)
    cos_t, sin_t = _rope_tables(pos, tm)
    head_dim = d // heads // 2
    u, q, kf, kb, vf, vb, gates = _inproj(x2, p['w_in'], cos_t, sin_t, tm=tm, q_scale=head_dim ** -0.5 * math.log2(math.e))

    out_scale = 1.0 - lam_init
    if k_past is None:
        attn = _attn_prompt(p['lam'], q.reshape(nb, t, d), kb.reshape(nb, t, d), vb.reshape(nb, t, d),
                            p['subln_g'], heads=heads, tq=min(512, t), out_scale=out_scale)
    else:
        past = k_past.shape[1]
        attn = _attn_sample(p['lam'], q.reshape(nb, t, d), k_past.reshape(nb, past, d), kb.reshape(nb, t, d),
                            v_past.reshape(nb, past, d), vb.reshape(nb, t, d), p['subln_g'],
                            heads=heads, out_scale=out_scale)

    y, f_re, f_im = _s5_branch(u, s_re, s_im, p['s5'], nb=nb, t=t, g=g, c=d // g, chunk=S5_CHUNK)

    h, hb, route = _post(attn.reshape(n, d), y, gates, x2, p['w_attn_branch'], p['w_glu'], p['w_out'],
                         p['ln1_g'], p['ln1_b'], p['wr_hi'], p['wr_lo'],
                         tm=tm, alpha=alpha, n_groups=p['n_groups'], epg=p['epg'])
    out = _moe(hb, h, route, p['w1'], p['w3'], p['w2'], p['ln2_g'], p['ln2_b'], tm=min(1024, n), alpha=alpha)
    k_new = kf.reshape(nb, t, heads, 2, head_dim)
    v_new = vf.reshape(nb, t, heads, 2 * head_dim)
    return out.reshape(nb, t, d), k_new, v_new, f_re, f_im


def kernel(x_prompt, x_sample, cache_k, cache_v, state_ssm_re, state_ssm_im, w_in, lambda_re, lambda_im, log_step, b_re, b_im, c_re, c_im, d_skip, w_glu, lambda_q1, lambda_k1, lambda_q2, lambda_k2, subln_g, w_attn_branch, w_out, ln1_g, ln1_b, w_router_group, w_router_expert, w1, w3, w2, ln2_g, ln2_b):
    depth = w_in.shape[0]
    d = x_prompt.shape[-1]
    heads = cache_k.shape[3]
    past_len = cache_k.shape[2]
    groups, pstate = lambda_re.shape[1], lambda_re.shape[2]
    n_groups, epg = w_router_expert.shape[2], w_router_expert.shape[3]
    n_exp = n_groups * epg
    alpha = (2.0 * depth) ** 0.25
    pos_prompt = jnp.arange(x_prompt.shape[1], dtype=jnp.int32)
    pos_sample = past_len + jnp.arange(x_sample.shape[1], dtype=jnp.int32)

    h_p, h_s = x_prompt, x_sample
    outs = [[] for _ in range(8)]
    for l in range(depth):
        lam_init = 0.8 - 0.6 * math.exp(-0.3 * l)
        lam = (jnp.exp(jnp.sum(lambda_q1[l].astype(F32) * lambda_k1[l].astype(F32)))
               - jnp.exp(jnp.sum(lambda_q2[l].astype(F32) * lambda_k2[l].astype(F32))) + lam_init)
        wr = jnp.concatenate([w_router_expert[l].reshape(d, n_exp).astype(F32), w_router_group[l].astype(F32),
                              jnp.zeros((d, LANES - n_exp - n_groups), F32)], axis=1)
        wr_hi = wr.astype(BF16)
        p = dict(
            heads=heads, groups=groups, pstate=pstate, n_groups=n_groups, epg=epg,
            w_in=w_in[l].astype(BF16), lam=lam.reshape(1).astype(F32),
            subln_g=subln_g[l].astype(F32).reshape(1, -1),
            s5=_s5_operators(lambda_re[l], lambda_im[l], log_step[l], b_re[l].astype(F32), b_im[l].astype(F32),
                             c_re[l].astype(F32), c_im[l].astype(F32), d_skip[l], S5_CHUNK),
            w_attn_branch=w_attn_branch[l].astype(BF16), w_glu=w_glu[l].astype(BF16), w_out=w_out[l].astype(BF16),
            ln1_g=ln1_g[l].astype(F32).reshape(1, d), ln1_b=ln1_b[l].astype(F32).reshape(1, d),
            ln2_g=ln2_g[l].astype(F32).reshape(1, d), ln2_b=ln2_b[l].astype(F32).reshape(1, d),
            wr_hi=wr_hi, wr_lo=(wr - wr_hi.astype(F32)).astype(BF16),
            w1=w1[l].astype(BF16), w3=w3[l].astype(BF16), w2=w2[l].astype(BF16),
        )
        zero_state = jnp.zeros((h_p.shape[0], groups, pstate), F32)
        h_p, k_p, v_p, sr_p, si_p = _encoder_layer(h_p, pos_prompt, None, None, zero_state, zero_state, p, lam_init, alpha)
        h_s, k_s, v_s, sr_s, si_s = _encoder_layer(h_s, pos_sample, cache_k[l], cache_v[l],
                                                   state_ssm_re[l].astype(F32), state_ssm_im[l].astype(F32),
                                                   p, lam_init, alpha)
        for lst, val in zip(outs, (k_p.astype(cache_k.dtype), v_p.astype(cache_v.dtype),
                                   sr_p.astype(state_ssm_re.dtype), si_p.astype(state_ssm_im.dtype),
                                   k_s.astype(cache_k.dtype), v_s.astype(cache_v.dtype),
                                   sr_s.astype(state_ssm_re.dtype), si_s.astype(state_ssm_im.dtype))):
            lst.append(val)
    return (h_p, h_s) + tuple(jnp.stack(o) for o in outs)
```

```python
import functools
import math

import jax
import jax.numpy as jnp
from jax import lax
from jax.experimental import pallas as pl
from jax.experimental.pallas import tpu as pltpu

F32 = jnp.float32
BF16 = jnp.bfloat16

LANES = 128
VMEM_LIMIT = 56 * 1024 * 1024

ATTN_CHUNK = 64
ROPE_THETA = 10000.0
MASK_VALUE = -1e30
SSM_MAX_RE = -1e-4
LN_EPS = 1e-5
S5_CHUNK = 32


def _cparams(sem):
    return pltpu.CompilerParams(dimension_semantics=sem, vmem_limit_bytes=VMEM_LIMIT)


def _const_spec(shape):
    nd = len(shape)
    return pl.BlockSpec(shape, lambda *_: (0,) * nd)


def _layer_norm(z, g, b):
    mu = jnp.mean(z, axis=-1, keepdims=True)
    zc = z - mu
    var = jnp.mean(zc * zc, axis=-1, keepdims=True)
    return zc * lax.rsqrt(var + LN_EPS) * g + b


def _sigmoid(x):
    return 1.0 / (1.0 + jnp.exp(-x))


def _inproj_body(x_ref, w_ref, cos_ref, sin_ref, u_ref, q_ref, kf_ref, kb_ref, vf_ref, vb_ref, g_ref,
                 *, d, q_scale):
    x = x_ref[...].astype(BF16)
    cos = cos_ref[...]
    sin = sin_ref[...]
    lane = lax.broadcasted_iota(jnp.int32, cos.shape, 1)
    first_half = (lane % 64) < 32

    def proj(j):
        return jnp.dot(x, w_ref[:, j * d:(j + 1) * d], preferred_element_type=F32)

    def rope(blk):
        outs = []
        for c in range(d // LANES):
            sub = blk[:, c * LANES:(c + 1) * LANES]
            partner = jnp.where(first_half, pltpu.roll(sub, 96, 1), pltpu.roll(sub, 32, 1))
            outs.append(sub * cos + partner * sin)
        return jnp.concatenate(outs, axis=1)

    u_ref[...] = proj(0).astype(BF16)
    q_ref[...] = (rope(proj(1)) * q_scale).astype(BF16)
    k = rope(proj(2))
    kf_ref[...] = k
    kb_ref[...] = k.astype(BF16)
    v = proj(3)
    vf_ref[...] = v
    vb_ref[...] = v.astype(BF16)
    g_ref[:, :d] = proj(4).astype(BF16)
    g_ref[:, d:] = proj(5).astype(BF16)


def _inproj(x2, w_bf, cos_t, sin_t, *, tm, q_scale):
    n, d = x2.shape
    width = w_bf.shape[1]
    n_tab = cos_t.shape[0] // tm
    row = lambda i: (i, 0)
    tab = lambda i: (i % n_tab, 0)
    outs = (
        jax.ShapeDtypeStruct((n, d), BF16),
        jax.ShapeDtypeStruct((n, d), BF16),
        jax.ShapeDtypeStruct((n, d), F32),
        jax.ShapeDtypeStruct((n, d), BF16),
        jax.ShapeDtypeStruct((n, d), F32),
        jax.ShapeDtypeStruct((n, d), BF16),
        jax.ShapeDtypeStruct((n, 2 * d), BF16),
    )
    return pl.pallas_call(
        functools.partial(_inproj_body, d=d, q_scale=q_scale),
        grid=(n // tm,),
        in_specs=[
            pl.BlockSpec((tm, d), row),
            pl.BlockSpec((d, width), lambda i: (0, 0), pipeline_mode=pl.Buffered(1)),
            pl.BlockSpec((tm, LANES), tab),
            pl.BlockSpec((tm, LANES), tab),
        ],
        out_specs=(
            pl.BlockSpec((tm, d), row), pl.BlockSpec((tm, d), row), pl.BlockSpec((tm, d), row),
            pl.BlockSpec((tm, d), row), pl.BlockSpec((tm, d), row), pl.BlockSpec((tm, d), row),
            pl.BlockSpec((tm, 2 * d), row),
        ),
        out_shape=outs,
        compiler_params=_cparams(("parallel",)),
        name="inproj_rope",
    )(x2, w_bf, cos_t, sin_t)


def _s5_state_body(u_ref, w_ref, zre_ref, zim_ref):
    u = jnp.concatenate([u_ref[0], u_ref[1]], axis=1)
    z = jnp.dot(u, w_ref[...], preferred_element_type=F32)
    zre_ref[...] = z[:, :LANES]
    zim_ref[...] = z[:, LANES:]


def _s5_state(u_g, w_state):
    gp, _, r, cw = u_g.shape
    return pl.pallas_call(
        _s5_state_body,
        grid=(gp,),
        in_specs=[
            pl.BlockSpec((None, 2, r, cw), lambda g: (g, 0, 0, 0)),
            pl.BlockSpec((None, 2 * cw, 2 * LANES), lambda g: (g, 0, 0)),
        ],
        out_specs=(pl.BlockSpec((r, LANES), lambda g: (0, g)), pl.BlockSpec((r, LANES), lambda g: (0, g))),
        out_shape=(jax.ShapeDtypeStruct((r, gp * LANES), F32), jax.ShapeDtypeStruct((r, gp * LANES), F32)),
        compiler_params=_cparams(("parallel",)),
        name="s5_chunk_state",
    )(u_g, w_state)


def _s5_scan_body(zre_ref, zim_ref, are_ref, aim_ref, s0re_ref, s0im_ref,
                  sre_ref, sim_ref, fre_ref, fim_ref, *, n_c):
    ar = are_ref[...]
    ai = aim_ref[...]

    def step(c, carry):
        s_re, s_im = carry
        sre_ref[c] = s_re
        sim_ref[c] = s_im
        n_re = ar * s_re - ai * s_im + zre_ref[c]
        n_im = ar * s_im + ai * s_re + zim_ref[c]
        return n_re, n_im

    f_re, f_im = lax.fori_loop(0, n_c, step, (s0re_ref[...], s0im_ref[...]))
    fre_ref[...] = f_re
    fim_ref[...] = f_im


def _s5_scan(z_re, z_im, a_re, a_im, s0_re, s0_im, *, lane_blk=256):
    n_c, nb, w = z_re.shape
    seq = pl.BlockSpec((n_c, nb, lane_blk), lambda j: (0, 0, j))
    vec = pl.BlockSpec((1, lane_blk), lambda j: (0, j))
    st = pl.BlockSpec((nb, lane_blk), lambda j: (0, j))
    return pl.pallas_call(
        functools.partial(_s5_scan_body, n_c=n_c),
        grid=(w // lane_blk,),
        in_specs=[seq, seq, vec, vec, st, st],
        out_specs=(seq, seq, st, st),
        out_shape=(jax.ShapeDtypeStruct((n_c, nb, w), F32), jax.ShapeDtypeStruct((n_c, nb, w), F32),
                   jax.ShapeDtypeStruct((nb, w), F32), jax.ShapeDtypeStruct((nb, w), F32)),
        compiler_params=_cparams(("parallel",)),
        name="s5_chunk_scan",
    )(z_re, z_im, a_re, a_im, s0_re, s0_im)


def _s5_out_body(u_ref, mt_ref, sre_ref, sim_ref, vre_ref, vim_ref, y_ref):
    s_re = sre_ref[...].astype(BF16)
    s_im = sim_ref[...].astype(BF16)
    inter = (jnp.dot(s_re, vre_ref[...], preferred_element_type=F32)
             + jnp.dot(s_im, vim_ref[...], preferred_element_type=F32))
    cw = u_ref.shape[-1]
    for i in range(2):
        intra = jnp.dot(u_ref[i], mt_ref[i], preferred_element_type=F32)
        y_ref[i] = (intra + inter[:, i * cw:(i + 1) * cw]).astype(BF16)


def _s5_out(u_g, mt, s_re, s_im, v_re, v_im):
    gp, _, r, cw = u_g.shape
    return pl.pallas_call(
        _s5_out_body,
        grid=(gp,),
        in_specs=[
            pl.BlockSpec((None, 2, r, cw), lambda g: (g, 0, 0, 0)),
            pl.BlockSpec((None, 2, cw, cw), lambda g: (g, 0, 0, 0)),
            pl.BlockSpec((r, LANES), lambda g: (0, g)),
            pl.BlockSpec((r, LANES), lambda g: (0, g)),
            pl.BlockSpec((None, LANES, 2 * cw), lambda g: (g, 0, 0)),
            pl.BlockSpec((None, LANES, 2 * cw), lambda g: (g, 0, 0)),
        ],
        out_specs=pl.BlockSpec((None, 2, r, cw), lambda g: (g, 0, 0, 0)),
        out_shape=jax.ShapeDtypeStruct((gp, 2, r, cw), BF16),
        compiler_params=_cparams(("parallel",)),
        name="s5_toeplitz_out",
    )(u_g, mt, s_re, s_im, v_re, v_im)


def _s5_operators(lambda_re, lambda_im, log_step, b_re, b_im, c_re, c_im, d_skip, chunk):
    hp = lax.Precision.HIGHEST
    g, p, c = b_re.shape
    lam_re = jnp.minimum(lambda_re.astype(F32), SSM_MAX_RE)
    lam_im = lambda_im.astype(F32)
    step = jnp.exp(log_step.astype(F32))[:, None]
    dd = jnp.arange(chunk + 1, dtype=F32)
    mag = jnp.exp((lam_re * step)[..., None] * dd)
    ang = (lam_im * step)[..., None] * dd
    pw_re = mag * jnp.cos(ang)
    pw_im = mag * jnp.sin(ang)
    a_re, a_im = pw_re[..., 1], pw_im[..., 1]
    den = lam_re * lam_re + lam_im * lam_im
    k_re = ((a_re - 1.0) * lam_re + a_im * lam_im) / den
    k_im = (a_im * lam_re - (a_re - 1.0) * lam_im) / den
    bb_re = k_re[..., None] * b_re - k_im[..., None] * b_im
    bb_im = k_re[..., None] * b_im + k_im[..., None] * b_re
    e_re = c_re[:, None] * jnp.moveaxis(pw_re, 2, 1)[:, :, None, :] - c_im[:, None] * jnp.moveaxis(pw_im, 2, 1)[:, :, None, :]
    e_im = c_re[:, None] * jnp.moveaxis(pw_im, 2, 1)[:, :, None, :] + c_im[:, None] * jnp.moveaxis(pw_re, 2, 1)[:, :, None, :]
    kern = (jnp.einsum('gdcp,gpk->gdck', e_re[:, :chunk], bb_re, precision=hp)
            - jnp.einsum('gdcp,gpk->gdck', e_im[:, :chunk], bb_im, precision=hp))
    kern = kern.at[:, 0].add(d_skip.astype(F32)[:, :, None] * jnp.eye(c, dtype=F32))
    idx = jnp.arange(chunk)
    lag = idx[None, :] - idx[:, None]
    kt = jnp.swapaxes(kern, 2, 3)
    mt = jnp.where((lag >= 0)[None, :, :, None, None], kt[:, jnp.clip(lag, 0, chunk - 1)], 0.0)
    mt = jnp.transpose(mt, (0, 1, 3, 2, 4)).reshape(g, chunk * c, chunk * c)
    rv_re = jnp.moveaxis(pw_re[..., :chunk][..., ::-1], 2, 1)
    rv_im = jnp.moveaxis(pw_im[..., :chunk][..., ::-1], 2, 1)
    bt_re = jnp.swapaxes(bb_re, 1, 2)
    bt_im = jnp.swapaxes(bb_im, 1, 2)
    w_re = (rv_re[:, :, None] * bt_re[:, None] - rv_im[:, :, None] * bt_im[:, None]).reshape(g, chunk * c, p)
    w_im = (rv_re[:, :, None] * bt_im[:, None] + rv_im[:, :, None] * bt_re[:, None]).reshape(g, chunk * c, p)
    v_re = jnp.transpose(e_re[:, 1:], (0, 3, 1, 2)).reshape(g, p, chunk * c)
    v_im = -jnp.transpose(e_im[:, 1:], (0, 3, 1, 2)).reshape(g, p, chunk * c)
    gp, cw = g // 2, chunk * c
    z = jnp.zeros((gp, cw, p), F32)
    wr = w_re.reshape(gp, 2, cw, p)
    wi = w_im.reshape(gp, 2, cw, p)
    w_state = jnp.concatenate([
        jnp.concatenate([wr[:, 0], z, wi[:, 0], z], axis=2),
        jnp.concatenate([z, wr[:, 1], z, wi[:, 1]], axis=2)], axis=1)
    zv = jnp.zeros((gp, p, cw), F32)
    vr = v_re.reshape(gp, 2, p, cw)
    vi = v_im.reshape(gp, 2, p, cw)
    v_re_pair = jnp.concatenate([jnp.concatenate([vr[:, 0], zv], axis=2),
                                 jnp.concatenate([zv, vr[:, 1]], axis=2)], axis=1)
    v_im_pair = jnp.concatenate([jnp.concatenate([vi[:, 0], zv], axis=2),
                                 jnp.concatenate([zv, vi[:, 1]], axis=2)], axis=1)
    al_re = pw_re[..., chunk].reshape(1, g * p)
    al_im = pw_im[..., chunk].reshape(1, g * p)
    return dict(mt=mt.reshape(gp, 2, cw, cw).astype(BF16), w_state=w_state.astype(BF16),
                v_re=v_re_pair.astype(BF16), v_im=v_im_pair.astype(BF16), al_re=al_re, al_im=al_im)


def _s5_branch(u2, s0_re, s0_im, ops, *, nb, t, g, c, chunk):
    n_c = t // chunk
    gp, cw = g // 2, chunk * c
    r = n_c * nb
    u_g = u2.reshape(nb, n_c, chunk, gp, 2, c)
    u_g = jnp.transpose(u_g, (3, 4, 1, 0, 2, 5)).reshape(gp, 2, r, cw)
    z_re, z_im = _s5_state(u_g, ops['w_state'])
    p = s0_re.shape[-1]
    w = g * p
    s_re, s_im, f_re, f_im = _s5_scan(z_re.reshape(n_c, nb, w), z_im.reshape(n_c, nb, w),
                                      ops['al_re'], ops['al_im'], s0_re.reshape(nb, w), s0_im.reshape(nb, w))
    y_g = _s5_out(u_g, ops['mt'], s_re.reshape(r, w), s_im.reshape(r, w), ops['v_re'], ops['v_im'])
    y = jnp.transpose(y_g.reshape(gp, 2, n_c, nb, chunk, c), (3, 2, 4, 0, 1, 5)).reshape(nb * t, g * c)
    return y, f_re.reshape(nb, g, p), f_im.reshape(nb, g, p)


def _split_heads_q(q):
    lane = lax.broadcasted_iota(jnp.int32, q.shape, 1)
    zero = jnp.zeros_like(q)
    return jnp.concatenate([jnp.where(lane < 64, q, zero), jnp.where(lane >= 64, q, zero)], axis=0)


def _diff_finish(acc, l, lam, g, out_scale, tq):
    o = acc[:tq] / l[:tq] - lam * (acc[tq:] / l[tq:])
    ms = jnp.mean(o * o, axis=-1, keepdims=True)
    return o * lax.rsqrt(ms + LN_EPS) * g * out_scale


def _attn_prompt_body(lam_ref, q_ref, k_ref, v_ref, g_ref, o_ref, m_sc, acc_sc, *, tq, rc, out_scale):
    qi = pl.program_id(2)
    hw = q_ref.shape[1]
    qz = _split_heads_q(q_ref[...])
    m_sc[...] = jnp.full(m_sc.shape, MASK_VALUE, F32)
    acc_sc[...] = jnp.zeros(acc_sc.shape, F32)
    ones = jnp.ones((tq, hw), BF16)
    dn = (((1,), (1,)), ((), ()))

    def block(kb, masked):
        off = pl.multiple_of(kb * tq, tq)
        k = k_ref[pl.ds(off, tq), :]
        v_ext = jnp.concatenate([v_ref[pl.ds(off, tq), :], ones], axis=1)
        rcb = min(2 * rc, tq) if masked else rc
        for c in range(2 * tq // rcb):
            rows = slice(c * rcb, (c + 1) * rcb)
            r0 = (c * rcb) % tq
            width = r0 + rcb if masked else tq
            s = lax.dot_general(qz[rows], k[:width], dn, preferred_element_type=F32)
            if masked:
                row = lax.broadcasted_iota(jnp.int32, s.shape, 0) + r0
                col = lax.broadcasted_iota(jnp.int32, s.shape, 1)
                s = jnp.where((col // ATTN_CHUNK) <= (row // ATTN_CHUNK), s, MASK_VALUE)
            m_prev = m_sc[rows]
            m_new = jnp.maximum(m_prev, jnp.max(s, axis=1, keepdims=True))
            alpha = jnp.exp2(m_prev - m_new)
            p = jnp.exp2(s - jnp.concatenate([m_new] * (width // hw), axis=1))
            acc_sc[rows] = (jnp.concatenate([alpha, alpha], axis=1) * acc_sc[rows]
                            + jnp.dot(p.astype(BF16), v_ext[:width], preferred_element_type=F32))
            m_sc[rows] = m_new

    def full_pair(j, carry):
        block(2 * j, False)
        block(2 * j + 1, False)
        return carry

    lax.fori_loop(0, qi // 2, full_pair, 0)

    @pl.when(qi % 2 == 1)
    def _():
        block(qi - 1, False)

    block(qi, True)
    acc = acc_sc[...]
    o_ref[...] = _diff_finish(acc[:, :hw], acc[:, hw:], lam_ref[0], g_ref[...], out_scale, tq).astype(BF16)


def _attn_prompt(lam, q3, k3, v3, g_row, *, heads, tq, out_scale):
    b, t, d = q3.shape
    hw = d // heads
    return pl.pallas_call(
        functools.partial(_attn_prompt_body, tq=tq, rc=min(128, tq), out_scale=out_scale),
        grid=(b, heads, t // tq),
        in_specs=[
            pl.BlockSpec(memory_space=pltpu.SMEM),
            pl.BlockSpec((None, tq, hw), lambda bi, h, qi: (bi, qi, h)),
            pl.BlockSpec((None, t, hw), lambda bi, h, qi: (bi, 0, h)),
            pl.BlockSpec((None, t, hw), lambda bi, h, qi: (bi, 0, h)),
            pl.BlockSpec((1, hw), lambda bi, h, qi: (0, 0)),
        ],
        out_specs=pl.BlockSpec((None, tq, hw), lambda bi, h, qi: (bi, qi, h)),
        out_shape=jax.ShapeDtypeStruct((b, t, d), BF16),
        scratch_shapes=[pltpu.VMEM((2 * tq, hw), F32), pltpu.VMEM((2 * tq, 2 * hw), F32)],
        compiler_params=_cparams(("parallel", "parallel", "arbitrary")),
        name="diff_attn_prompt",
    )(lam, q3, k3, v3, g_row)


def _attn_sample_body(lam_ref, q_ref, kc_ref, kn_ref, vc_ref, vn_ref, g_ref, o_ref, *, heads, out_scale):
    tq = q_ref.shape[0]
    hw = q_ref.shape[1] // heads
    dn = (((1,), (1,)), ((), ()))
    for h in range(heads):
        sl = slice(h * hw, (h + 1) * hw)
        qz = _split_heads_q(q_ref[:, sl])
        s_c = lax.dot_general(qz, kc_ref[:, sl].astype(BF16), dn, preferred_element_type=F32)
        s_n = lax.dot_general(qz, kn_ref[:, sl], dn, preferred_element_type=F32)
        m = jnp.maximum(jnp.max(s_c, axis=1, keepdims=True), jnp.max(s_n, axis=1, keepdims=True))
        p_c = jnp.exp2(s_c - m)
        p_n = jnp.exp2(s_n - m)
        l = jnp.sum(p_c, axis=1, keepdims=True) + jnp.sum(p_n, axis=1, keepdims=True)
        acc = (jnp.dot(p_c.astype(BF16), vc_ref[:, sl].astype(BF16), preferred_element_type=F32)
               + jnp.dot(p_n.astype(BF16), vn_ref[:, sl], preferred_element_type=F32))
        o_ref[:, sl] = _diff_finish(acc, l, lam_ref[0], g_ref[...], out_scale, tq).astype(BF16)


def _attn_sample(lam, q3, kc3, kn3, vc3, vn3, g_row, *, heads, out_scale):
    b, t, d = q3.shape
    past = kc3.shape[1]
    new = pl.BlockSpec((None, t, d), lambda bi: (bi, 0, 0))
    old = pl.BlockSpec((None, past, d), lambda bi: (bi, 0, 0))
    return pl.pallas_call(
        functools.partial(_attn_sample_body, heads=heads, out_scale=out_scale),
        grid=(b,),
        in_specs=[pl.BlockSpec(memory_space=pltpu.SMEM), new, old, new, old, new,
                  pl.BlockSpec((1, d // heads), lambda bi: (0, 0))],
        out_specs=new,
        out_shape=jax.ShapeDtypeStruct((b, t, d), BF16),
        compiler_params=_cparams(("parallel",)),
        name="diff_attn_sample",
    )(lam, q3, kc3, kn3, vc3, vn3, g_row)


def _route(logits, n_groups, epg):
    n_exp = n_groups * epg
    lane = lax.broadcasted_iota(jnp.int32, logits.shape, 1)
    neg = jnp.float32(-jnp.inf)
    is_g = (lane >= n_exp) & (lane < n_exp + n_groups)
    gl = jnp.where(is_g, logits, neg)
    gmax = jnp.max(gl, axis=1, keepdims=True)
    g_idx = jnp.min(jnp.where(gl == gmax, lane - n_exp, n_groups), axis=1, keepdims=True)
    g_w = 1.0 / jnp.sum(jnp.where(is_g, jnp.exp(gl - gmax), 0.0), axis=1, keepdims=True)
    in_grp = (lane < n_exp) & ((lane // epg) == g_idx)
    el = jnp.where(in_grp, logits, neg)
    t1 = jnp.max(el, axis=1, keepdims=True)
    i1 = jnp.min(jnp.where(el == t1, lane, LANES), axis=1, keepdims=True)
    el2 = jnp.where(lane == i1, neg, el)
    t2 = jnp.max(el2, axis=1, keepdims=True)
    i2 = jnp.min(jnp.where(el2 == t2, lane, LANES), axis=1, keepdims=True)
    e2 = jnp.exp(t2 - t1)
    w1 = g_w / (1.0 + e2)
    w2 = g_w * e2 / (1.0 + e2)
    return jnp.where(lane == i1, w1, 0.0) + jnp.where(lane == i2, w2, 0.0)


def _post_body(attn_ref, y_ref, gates_ref, x_ref, wa_ref, wglu_ref, wout_ref, lng_ref, lnb_ref,
               wrh_ref, wrl_ref, h_ref, hb_ref, route_ref, *, d, alpha, n_groups, epg):
    b_attn = jnp.dot(attn_ref[...], wa_ref[...], preferred_element_type=F32)
    yg = jax.nn.gelu(y_ref[...].astype(F32)).astype(BF16)
    glu = jnp.dot(yg, wglu_ref[...], preferred_element_type=F32)
    b_ssm = glu[:, :d] * _sigmoid(glu[:, d:])
    merged = (_sigmoid(gates_ref[:, :d].astype(F32)) * b_ssm
              + _sigmoid(gates_ref[:, d:].astype(F32)) * b_attn)
    mix = jnp.dot(merged.astype(BF16), wout_ref[...], preferred_element_type=F32)
    h = _layer_norm(alpha * x_ref[...] + mix, lng_ref[...], lnb_ref[...])
    h_ref[...] = h
    h_hi = h.astype(BF16)
    hb_ref[...] = h_hi
    h_lo = (h - h_hi.astype(F32)).astype(BF16)
    logits = (jnp.dot(h_hi, wrh_ref[...], preferred_element_type=F32)
              + jnp.dot(h_lo, wrh_ref[...], preferred_element_type=F32)
              + jnp.dot(h_hi, wrl_ref[...], preferred_element_type=F32))
    route_ref[...] = _route(logits, n_groups, epg)


def _post(attn2, y2, gates2, x2, wa, wglu, wout, lng, lnb, wr_hi, wr_lo, *, tm, alpha, n_groups, epg):
    n, d = x2.shape
    row = lambda i: (i, 0)
    wspec = lambda shape: pl.BlockSpec(shape, lambda i: (0, 0), pipeline_mode=pl.Buffered(1))
    return pl.pallas_call(
        functools.partial(_post_body, d=d, alpha=alpha, n_groups=n_groups, epg=epg),
        grid=(n // tm,),
        in_specs=[
            pl.BlockSpec((tm, d), row), pl.BlockSpec((tm, d), row), pl.BlockSpec((tm, 2 * d), row),
            pl.BlockSpec((tm, d), row),
            wspec((d, d)), wspec((d, 2 * d)), wspec((d, d)),
            _const_spec((1, d)), _const_spec((1, d)),
            wspec((d, LANES)), wspec((d, LANES)),
        ],
        out_specs=(pl.BlockSpec((tm, d), row), pl.BlockSpec((tm, d), row), pl.BlockSpec((tm, LANES), row)),
        out_shape=(jax.ShapeDtypeStruct((n, d), F32), jax.ShapeDtypeStruct((n, d), BF16),
                   jax.ShapeDtypeStruct((n, LANES), F32)),
        compiler_params=_cparams(("parallel",)),
        name="merge_outproj_ln_router",
    )(attn2, y2, gates2, x2, wa, wglu, wout, lng, lnb, wr_hi, wr_lo)


def _moe_body(hb_ref, h_ref, route_ref, w1_ref, w3_ref, w2_ref, lng_ref, lnb_ref, o_ref, acc_sc, *, alpha):
    e = pl.program_id(1)

    @pl.when(e == 0)
    def _():
        acc_sc[...] = jnp.zeros(acc_sc.shape, F32)

    x = hb_ref[...]
    a = jnp.dot(x, w1_ref[...], preferred_element_type=F32)
    b = jnp.dot(x, w3_ref[...], preferred_element_type=F32)
    route = route_ref[...]
    lane = lax.broadcasted_iota(jnp.int32, route.shape, 1)
    gate = jnp.sum(jnp.where(lane == e, route, 0.0), axis=1, keepdims=True)
    mid = (a * _sigmoid(a) * b * gate).astype(BF16)
    acc_sc[...] += jnp.dot(mid, w2_ref[...], preferred_element_type=F32)

    @pl.when(e == pl.num_programs(1) - 1)
    def _():
        o_ref[...] = _layer_norm(alpha * h_ref[...] + acc_sc[...], lng_ref[...], lnb_ref[...])


def _moe(hb, h, route, w1, w3, w2, lng, lnb, *, tm, alpha):
    n, d = h.shape
    n_exp, _, de = w1.shape
    row = lambda i, e: (i, 0)
    return pl.pallas_call(
        functools.partial(_moe_body, alpha=alpha),
        grid=(n // tm, n_exp),
        in_specs=[
            pl.BlockSpec((tm, d), row), pl.BlockSpec((tm, d), row), pl.BlockSpec((tm, LANES), row),
            pl.BlockSpec((None, d, de), lambda i, e: (e, 0, 0)),
            pl.BlockSpec((None, d, de), lambda i, e: (e, 0, 0)),
            pl.BlockSpec((None, de, d), lambda i, e: (e, 0, 0)),
            pl.BlockSpec((1, d), lambda i, e: (0, 0)), pl.BlockSpec((1, d), lambda i, e: (0, 0)),
        ],
        out_specs=pl.BlockSpec((tm, d), row),
        out_shape=jax.ShapeDtypeStruct((n, d), F32),
        scratch_shapes=[pltpu.VMEM((tm, d), F32)],
        compiler_params=_cparams(("parallel", "arbitrary")),
        name="moe_experts_ln",
    )(hb, h, route, w1, w3, w2, lng, lnb)


def _rope_tables(pos, rows):
    half = 32
    inv_freq = ROPE_THETA ** (-jnp.arange(half, dtype=F32) / half)
    ang = pos.astype(F32)[:, None] * inv_freq[None, :]
    cos = jnp.cos(ang)
    sin = jnp.sin(ang)
    cos_t = jnp.tile(jnp.concatenate([cos, cos], axis=1), (1, LANES // 64))
    sin_t = jnp.tile(jnp.concatenate([-sin, sin], axis=1), (1, LANES // 64))
    reps = max(1, rows // cos_t.shape[0])
    return jnp.tile(cos_t, (reps, 1)), jnp.tile(sin_t, (reps, 1))


def _encoder_layer(x, pos, k_past, v_past, s_re, s_im, p, lam_init, alpha):
    nb, t, d = x.shape
    n = nb * t
    heads = p['heads']
    g, pstate = p['groups'], p['pstate']
    tm = min(512, n)
    x2 = x.reshape(n, d)
    cos_t, sin_t = _rope_tables(pos, tm)
    head_dim = d // heads // 2
    u, q, kf, kb, vf, vb, gates = _inproj(x2, p['w_in'], cos_t, sin_t, tm=tm, q_scale=head_dim ** -0.5 * math.log2(math.e))

    out_scale = 1.0 - lam_init
    if k_past is None:
        attn = _attn_prompt(p['lam'], q.reshape(nb, t, d), kb.reshape(nb, t, d), vb.reshape(nb, t, d),
                            p['subln_g'], heads=heads, tq=min(512, t), out_scale=out_scale)
    else:
        past = k_past.shape[1]
        attn = _attn_sample(p['lam'], q.reshape(nb, t, d), k_past.reshape(nb, past, d), kb.reshape(nb, t, d),
                            v_past.reshape(nb, past, d), vb.reshape(nb, t, d), p['subln_g'],
                            heads=heads, out_scale=out_scale)

    y, f_re, f_im = _s5_branch(u, s_re, s_im, p['s5'], nb=nb, t=t, g=g, c=d // g, chunk=S5_CHUNK)

    h, hb, route = _post(attn.reshape(n, d), y, gates, x2, p['w_attn_branch'], p['w_glu'], p['w_out'],
                         p['ln1_g'], p['ln1_b'], p['wr_hi'], p['wr_lo'],
                         tm=tm, alpha=alpha, n_groups=p['n_groups'], epg=p['epg'])
    out = _moe(hb, h, route, p['w1'], p['w3'], p['w2'], p['ln2_g'], p['ln2_b'], tm=min(1024, n), alpha=alpha)
    k_new = kf.reshape(nb, t, heads, 2, head_dim)
    v_new = vf.reshape(nb, t, heads, 2 * head_dim)
    return out.reshape(nb, t, d), k_new, v_new, f_re, f_im


def kernel(x_prompt, x_sample, cache_k, cache_v, state_ssm_re, state_ssm_im, w_in, lambda_re, lambda_im, log_step, b_re, b_im, c_re, c_im, d_skip, w_glu, lambda_q1, lambda_k1, lambda_q2, lambda_k2, subln_g, w_attn_branch, w_out, ln1_g, ln1_b, w_router_group, w_router_expert, w1, w3, w2, ln2_g, ln2_b):
    depth = w_in.shape[0]
    d = x_prompt.shape[-1]
    heads = cache_k.shape[3]
    past_len = cache_k.shape[2]
    groups, pstate = lambda_re.shape[1], lambda_re.shape[2]
    n_groups, epg = w_router_expert.shape[2], w_router_expert.shape[3]
    n_exp = n_groups * epg
    alpha = (2.0 * depth) ** 0.25
    pos_prompt = jnp.arange(x_prompt.shape[1], dtype=jnp.int32)
    pos_sample = past_len + jnp.arange(x_sample.shape[1], dtype=jnp.int32)

    h_p, h_s = x_prompt, x_sample
    outs = [[] for _ in range(8)]
    for l in range(depth):
        lam_init = 0.8 - 0.6 * math.exp(-0.3 * l)
        lam = (jnp.exp(jnp.sum(lambda_q1[l].astype(F32) * lambda_k1[l].astype(F32)))
               - jnp.exp(jnp.sum(lambda_q2[l].astype(F32) * lambda_k2[l].astype(F32))) + lam_init)
        wr = jnp.concatenate([w_router_expert[l].reshape(d, n_exp).astype(F32), w_router_group[l].astype(F32),
                              jnp.zeros((d, LANES - n_exp - n_groups), F32)], axis=1)
        wr_hi = wr.astype(BF16)
        p = dict(
            heads=heads, groups=groups, pstate=pstate, n_groups=n_groups, epg=epg,
            w_in=w_in[l].astype(BF16), lam=lam.reshape(1).astype(F32),
            subln_g=subln_g[l].astype(F32).reshape(1, -1),
            s5=_s5_operators(lambda_re[l], lambda_im[l], log_step[l], b_re[l].astype(F32), b_im[l].astype(F32),
                             c_re[l].astype(F32), c_im[l].astype(F32), d_skip[l], S5_CHUNK),
            w_attn_branch=w_attn_branch[l].astype(BF16), w_glu=w_glu[l].astype(BF16), w_out=w_out[l].astype(BF16),
            ln1_g=ln1_g[l].astype(F32).reshape(1, d), ln1_b=ln1_b[l].astype(F32).reshape(1, d),
            ln2_g=ln2_g[l].astype(F32).reshape(1, d), ln2_b=ln2_b[l].astype(F32).reshape(1, d),
            wr_hi=wr_hi, wr_lo=(wr - wr_hi.astype(F32)).astype(BF16),
            w1=w1[l].astype(BF16), w3=w3[l].astype(BF16), w2=w2[l].astype(BF16),
        )
        zero_state = jnp.zeros((h_p.shape[0], groups, pstate), F32)
        h_p, k_p, v_p, sr_p, si_p = _encoder_layer(h_p, pos_prompt, None, None, zero_state, zero_state, p, lam_init, alpha)
        h_s, k_s, v_s, sr_s, si_s = _encoder_layer(h_s, pos_sample, cache_k[l], cache_v[l],
                                                   state_ssm_re[l].astype(F32), state_ssm_im[l].astype(F32),
                                                   p, lam_init, alpha)
        for lst, val in zip(outs, (k_p.astype(cache_k.dtype), v_p.astype(cache_v.dtype),
                                   sr_p.astype(state_ssm_re.dtype), si_p.astype(state_ssm_im.dtype),
                                   k_s.astype(cache_k.dtype), v_s.astype(cache_v.dtype),
                                   sr_s.astype(state_ssm_re.dtype), si_s.astype(state_ssm_im.dtype))):
            lst.append(val)
    return (h_p, h_s) + tuple(jnp.stack(o) for o in outs)
```
